```python
import jax
import jax.numpy as jnp
from jax import lax
import numpy as np

D_MODEL = 1024
BATCH = 8
SEQ = 2048
DEPTH = 4
DEC_BATCH = 128
DEC_SEQ = 4
PAST_LEN = 8192
PAGE_SIZE = 128

HEAD_DIM = 64
N_HEADS_A = 8
N_KV_A = 2
WINDOW_A = 128
B_PAIRS = ((128, 1), (512, 4), (2048, 16))
N_GROUPS_B = 3
HEADS_PER_GROUP_B = 4
POOL_WINDOWS = (2, 4, 8, 16)
POOL_GROUP_WIDTH = 64
POOL_STATE = 15

WIDTH_A = N_HEADS_A * HEAD_DIM
WIDTH_B = HEADS_PER_GROUP_B * HEAD_DIM
WIDTH_C = len(POOL_WINDOWS) * POOL_GROUP_WIDTH
PROJ_SIZES = (WIDTH_A, N_KV_A * HEAD_DIM, N_KV_A * HEAD_DIM, WIDTH_A,
              N_GROUPS_B * WIDTH_B, N_GROUPS_B * WIDTH_B, N_GROUPS_B * WIDTH_B, WIDTH_B,
              WIDTH_C, WIDTH_C, 3 * D_MODEL)
D_IN = 2 * WIDTH_A + 2 * N_KV_A * HEAD_DIM + 3 * N_GROUPS_B * WIDTH_B + WIDTH_B + 2 * WIDTH_C + 3 * D_MODEL
BLOCK = 128
RMS_EPS = 1e-6
NEG_INF = -1e30

kernel_name = 'hybrid_gated_swa_dilated_pool_step'


def _rmsnorm(x, w):
    xf = x.astype(jnp.float32)
    y = xf * lax.rsqrt(jnp.mean(xf * xf, axis=-1, keepdims=True) + RMS_EPS)
    return (y * w.astype(jnp.float32)).astype(x.dtype)


def _alibi_slopes(n):
    return 2.0 ** (-8.0 * (jnp.arange(n, dtype=jnp.float32) + 1.0) / n)


def _probs_lse(s, sink):
    m = jnp.max(s, axis=-1, keepdims=True)
    if sink is not None:
        sk = sink.astype(jnp.float32)[..., None, None]
        m = jnp.maximum(m, sk)
    e = jnp.exp(s - m)
    z = jnp.sum(e, axis=-1, keepdims=True)
    if sink is not None:
        z = z + jnp.exp(sk - m)
    return e / z, (m + jnp.log(z))[..., 0]


def _band_attention(q, k, v, slopes, nwin, dist_scale, sink):
    n, L, G, R, hd = q.shape
    nb = -(-L // BLOCK)
    pad = nb * BLOCK - L
    qb = jnp.pad(q, ((0, 0), (0, pad), (0, 0), (0, 0), (0, 0))).reshape(n, nb, BLOCK, G, R, hd)

    def windows(t):
        tp = jnp.pad(t, ((0, 0), (BLOCK, pad), (0, 0), (0, 0))).reshape(n, nb + 1, BLOCK, G, hd)
        return jnp.concatenate([tp[:, :-1], tp[:, 1:]], axis=2)

    kw, vw = windows(k), windows(v)
    s = jnp.einsum('nbqgrd,nbkgd->nbgrqk', qb, kw).astype(jnp.float32) * (hd ** -0.5)
    qi = jnp.arange(BLOCK)[:, None]
    ki = jnp.arange(2 * BLOCK)[None, :]
    dist = BLOCK + qi - ki
    kpos = jnp.arange(nb)[:, None, None] * BLOCK - BLOCK + ki[None]
    valid = (dist >= 0) & (dist <= nwin) & (kpos >= 0)
    bias = -(slopes * dist_scale)[:, :, None, None] * dist.astype(jnp.float32)
    s = jnp.where(valid[:, None, None], s + bias, NEG_INF)
    p, lse = _probs_lse(s, sink)
    o = jnp.einsum('nbgrqk,nbkgd->nbqgrd', p.astype(v.dtype), vw).reshape(n, nb * BLOCK, G, R, hd)[:, :L]
    lse = lse.transpose(0, 1, 4, 2, 3).reshape(n, nb * BLOCK, G, R)[:, :L]
    return o, lse


def _dilated_prompt(q, k, v, slopes, window, dilation, sink):
    n, t_len = q.shape[:2]
    sub = t_len // dilation

    def split(t):
        return t.reshape(n, sub, dilation, *t.shape[2:]).swapaxes(1, 2).reshape(n * dilation, sub, *t.shape[2:])

    def merge(t):
        return t.reshape(n, dilation, sub, *t.shape[2:]).swapaxes(1, 2).reshape(n, t_len, *t.shape[2:])

    o, lse = _band_attention(split(q), split(k), split(v), slopes, window // dilation, dilation, sink)
    return merge(o), merge(lse)


def _dilated_sample(q, k_ext, v_ext, slopes, window, dilation, sink):
    n, n_q, G, R, hd = q.shape
    n_prev = k_ext.shape[1] - n_q
    step = jnp.arange(window // dilation + 1)
    idx = n_prev + jnp.arange(n_q)[:, None] - dilation * step[None, :]
    valid = (PAST_LEN - n_prev + idx) >= 0
    kg = k_ext[:, idx]
    vg = v_ext[:, idx]
    s = jnp.einsum('nqgrd,nqkgd->ngrqk', q, kg).astype(jnp.float32) * (hd ** -0.5)
    bias = -(slopes * dilation)[:, :, None, None] * step.astype(jnp.float32)
    s = jnp.where(valid, s + bias, NEG_INF)
    p, lse = _probs_lse(s, sink)
    o = jnp.einsum('ngrqk,nqkgd->nqgrd', p.astype(v_ext.dtype), vg)
    return o, lse.transpose(0, 3, 1, 2)


def _last_rows(t, n_rows, axis):
    size = t.shape[axis]
    if size < n_rows:
        pads = [(0, 0)] * t.ndim
        pads[axis] = (n_rows - size, 0)
        t = jnp.pad(t, pads)
    return lax.slice_in_dim(t, t.shape[axis] - n_rows, t.shape[axis], axis=axis)


def _pool_mix(u_ext, n_prev, pos0, w_pool, pool_scale):
    n, L, C = u_ext.shape
    t_len = L - n_prev
    win = jnp.repeat(jnp.array(POOL_WINDOWS, dtype=jnp.int32), POOL_GROUP_WIDTH)
    uf = u_ext.astype(jnp.float32)
    csum = jnp.concatenate([jnp.zeros((n, 1, C), jnp.float32), jnp.cumsum(uf, axis=1)], axis=1)
    hi = n_prev + 1 + jnp.arange(t_len)
    lo = jnp.maximum(hi[:, None] - win[None, :], 0)
    wsum = csum[:, hi] - jnp.take_along_axis(csum, jnp.broadcast_to(lo[None], (n, t_len, C)), axis=1)
    cnt = jnp.minimum(pos0 + 1 + jnp.arange(t_len)[:, None], win[None, :]).astype(jnp.float32)
    diff = (wsum / cnt - uf[:, n_prev:]).reshape(n, t_len, len(POOL_WINDOWS), POOL_GROUP_WIDTH)
    y = jnp.einsum('ntgc,gce->ntge', diff, w_pool.astype(jnp.float32)).reshape(n, t_len, C)
    return (y * pool_scale.astype(jnp.float32)).astype(u_ext.dtype)


def _layer(x, norm_w, w_in, sink_a, w_proj_a, w_proj_b, w_proj_c, w_pool, pool_scale, w_out, past):
    n, t_len, _ = x.shape
    z = _rmsnorm(x, norm_w) @ w_in
    split_at = np.cumsum(PROJ_SIZES)[:-1].tolist()
    qa, ka, va, ga, qb, kb, vb, gb, uc, gc, mg = jnp.split(z, split_at, axis=-1)
    rep_a = N_HEADS_A // N_KV_A
    qa = qa.reshape(n, t_len, N_KV_A, rep_a, HEAD_DIM)
    kva = jnp.stack([ka.reshape(n, t_len, N_KV_A, HEAD_DIM), va.reshape(n, t_len, N_KV_A, HEAD_DIM)], axis=1)
    qb = qb.reshape(n, t_len, N_GROUPS_B, HEADS_PER_GROUP_B, 1, HEAD_DIM)
    kvb = jnp.stack([kb.reshape(n, t_len, N_GROUPS_B, HEADS_PER_GROUP_B, HEAD_DIM),
                     vb.reshape(n, t_len, N_GROUPS_B, HEADS_PER_GROUP_B, HEAD_DIM)], axis=1)
    slopes_a = _alibi_slopes(N_HEADS_A).reshape(N_KV_A, rep_a)
    slopes_b = _alibi_slopes(N_GROUPS_B * HEADS_PER_GROUP_B).reshape(N_GROUPS_B, HEADS_PER_GROUP_B, 1)

    def extend(new_rows, i, axis):
        return new_rows if past is None else jnp.concatenate([past[i], new_rows], axis=axis)

    def attend(q, kv_ext, slopes, window, dilation, sink):
        fn = _dilated_prompt if past is None else _dilated_sample
        return fn(q, kv_ext[:, 0], kv_ext[:, 1], slopes, window, dilation, sink)

    kva_ext = extend(kva, 0, 2)
    oa, _ = attend(qa, kva_ext, slopes_a, WINDOW_A, 1, sink_a.reshape(N_KV_A, rep_a))
    new_a = _last_rows(kva_ext, WINDOW_A, 2)

    outs, lses, new_b = [], [], []
    for g, (win, dil) in enumerate(B_PAIRS):
        kv_ext = extend(kvb[:, :, :, g], 1 + g, 2)
        o_g, lse_g = attend(qb[:, :, g], kv_ext, slopes_b[g], win, dil, None)
        outs.append(o_g.astype(jnp.float32))
        lses.append(lse_g)
        new_b.append(_last_rows(kv_ext, win, 2))
    w_mix = jax.nn.softmax(jnp.stack(lses), axis=0)
    ob = jnp.sum(w_mix[..., None] * jnp.stack(outs), axis=0)

    u_ext = extend(uc, 4, 1)
    n_prev = 0 if past is None else POOL_STATE
    pos0 = 0 if past is None else PAST_LEN
    oc = _pool_mix(u_ext, n_prev, pos0, w_pool, pool_scale)
    new_c = _last_rows(u_ext, POOL_STATE, 1)

    ma, mb, mc = jnp.split(mg, 3, axis=-1)
    pa = (oa.reshape(n, t_len, WIDTH_A).astype(x.dtype) * jax.nn.silu(ga)) @ w_proj_a
    pb = (ob.reshape(n, t_len, WIDTH_B).astype(x.dtype) * jax.nn.silu(gb)) @ w_proj_b
    pc = (oc * jax.nn.silu(gc)) @ w_proj_c
    m = jax.nn.sigmoid(ma) * pa + jax.nn.sigmoid(mb) * pb + jax.nn.sigmoid(mc) * pc
    return x + m @ w_out, (new_a, new_b[0], new_b[1], new_b[2], new_c)


def _stack_layers(states, i):
    return jnp.stack([s[i] for s in states], axis=0)


def setup_inputs(seed: int = 0) -> dict:
    key = jax.random.key(seed)
    ks = jax.random.split(key, 17)

    def nrm(k, shape, scale):
        return jax.random.normal(k, shape, jnp.float32) * scale

    kv_b = [(DEPTH, DEC_BATCH, 2, w, HEADS_PER_GROUP_B, HEAD_DIM) for (w, _) in B_PAIRS]
    return {
        'x_prompt': nrm(ks[0], (BATCH, SEQ, D_MODEL), 1.0),
        'x_sample': nrm(ks[1], (DEC_BATCH, DEC_SEQ, D_MODEL), 1.0),
        'state_a': nrm(ks[2], (DEPTH, DEC_BATCH, 2, WINDOW_A, N_KV_A, HEAD_DIM), 1.0),
        'state_b1': nrm(ks[3], kv_b[0], 1.0),
        'state_b2': nrm(ks[4], kv_b[1], 1.0),
        'state_b3': nrm(ks[5], kv_b[2], 1.0),
        'state_c': nrm(ks[6], (DEPTH, DEC_BATCH, POOL_STATE, WIDTH_C), 1.0),
        'norm_w': 1.0 + nrm(ks[7], (DEPTH, D_MODEL), 0.1),
        'final_norm_w': 1.0 + nrm(ks[8], (D_MODEL,), 0.1),
        'w_in': nrm(ks[9], (DEPTH, D_MODEL, D_IN), D_MODEL ** -0.5),
        'sink_a': nrm(ks[10], (DEPTH, N_HEADS_A), 0.5),
        'w_proj_a': nrm(ks[11], (DEPTH, WIDTH_A, D_MODEL), WIDTH_A ** -0.5),
        'w_proj_b': nrm(ks[12], (DEPTH, WIDTH_B, D_MODEL), WIDTH_B ** -0.5),
        'w_proj_c': nrm(ks[13], (DEPTH, WIDTH_C, D_MODEL), WIDTH_C ** -0.5),
        'w_pool': nrm(ks[14], (DEPTH, len(POOL_WINDOWS), POOL_GROUP_WIDTH, POOL_GROUP_WIDTH), POOL_GROUP_WIDTH ** -0.5),
        'pool_scale': 1.0 + nrm(ks[15], (DEPTH, WIDTH_C), 0.1),
        'w_out': nrm(ks[16], (DEPTH, D_MODEL, D_MODEL), 0.5 * D_MODEL ** -0.5),
    }


def reference(x_prompt, x_sample, state_a, state_b1, state_b2, state_b3, state_c,
              norm_w, final_norm_w, w_in, sink_a, w_proj_a, w_proj_b, w_proj_c,
              w_pool, pool_scale, w_out):
    hp, hs = x_prompt, x_sample
    new_p, new_s = [], []
    for l in range(DEPTH):
        lw = (norm_w[l], w_in[l], sink_a[l], w_proj_a[l], w_proj_b[l], w_proj_c[l],
              w_pool[l], pool_scale[l], w_out[l])
        hp, sp = _layer(hp, *lw, None)
        hs, ss = _layer(hs, *lw, (state_a[l], state_b1[l], state_b2[l], state_b3[l], state_c[l]))
        new_p.append(sp)
        new_s.append(ss)
    y_prompt = _rmsnorm(hp, final_norm_w)
    y_sample = _rmsnorm(hs, final_norm_w)
    return (y_prompt, y_sample,
            _stack_layers(new_p, 0), _stack_layers(new_s, 0),
            _stack_layers(new_p, 1), _stack_layers(new_s, 1),
            _stack_layers(new_p, 2), _stack_layers(new_s, 2),
            _stack_layers(new_p, 3), _stack_layers(new_s, 3),
            _stack_layers(new_p, 4), _stack_layers(new_s, 4))
```

```python
import functools

import numpy as np
import jax
import jax.numpy as jnp
from jax import lax
from jax.experimental import pallas as pl
from jax.experimental.pallas import tpu as pltpu

F32 = jnp.float32
BF16 = jnp.bfloat16

D_MODEL = 1024
HEAD_DIM = 64
N_HEADS_A = 8
N_KV_A = 2
REP_A = N_HEADS_A // N_KV_A
WINDOW_A = 128
B_PAIRS = ((128, 1), (512, 4), (2048, 16))
N_GROUPS_B = 3
HEADS_B = 4
POOL_WINDOWS = (2, 4, 8, 16)
POOL_STATE = 15
WIDTH_A = N_HEADS_A * HEAD_DIM
WIDTH_KV_A = N_KV_A * HEAD_DIM
WIDTH_B = HEADS_B * HEAD_DIM
WIDTH_C = 256
BLOCK = 128
RMS_EPS = 1e-6
NEG_INF = -1e30
LANES = 128
SUBLANES = 8
VMEM_LIMIT = 56 * 1024 * 1024

SEG_A = 0
SEG_B = (768, 1536, 2304)
SEG_W = 768
SEG_G = 3072
GATE_W = 4096
SEG_U = 7168
D_IN = 7424
QKV_W = 3072


def _orig_segments():
    qa, ka, va, ga, qb, kb, vb, gb, uc, gc, mg = np.cumsum(
        [0, WIDTH_A, WIDTH_KV_A, WIDTH_KV_A, WIDTH_A, 3 * WIDTH_B, 3 * WIDTH_B, 3 * WIDTH_B, WIDTH_B, WIDTH_C, WIDTH_C])
    scale = HEAD_DIM ** -0.5
    runs = []

    def head_pairs(base, s):
        for p in range(REP_A):
            for g in range(N_KV_A):
                lo = base + (g * REP_A + p) * HEAD_DIM
                runs.append((lo, lo + HEAD_DIM, s))

    head_pairs(qa, scale)
    runs.append((ka, ka + WIDTH_KV_A, 1.0))
    runs.append((va, va + WIDTH_KV_A, 1.0))
    for g in range(N_GROUPS_B):
        runs.append((qb + g * WIDTH_B, qb + (g + 1) * WIDTH_B, scale))
        runs.append((kb + g * WIDTH_B, kb + (g + 1) * WIDTH_B, 1.0))
        runs.append((vb + g * WIDTH_B, vb + (g + 1) * WIDTH_B, 1.0))
    head_pairs(ga, 1.0)
    runs.append((gb, gb + WIDTH_B, 1.0))
    runs.append((gc, gc + WIDTH_C, 1.0))
    runs.append((mg, mg + 3 * D_MODEL, 1.0))
    runs.append((uc, uc + WIDTH_C, 1.0))
    return runs


def _alibi_slopes(n):
    return 2.0 ** (-8.0 * (np.arange(n, dtype=np.float32) + 1.0) / n)


SLOPES_A = _alibi_slopes(N_HEADS_A)
SLOPES_B = _alibi_slopes(N_GROUPS_B * HEADS_B)


def _band_table(slope, dist_scale):
    qi = np.arange(BLOCK)[:, None]
    ki = np.arange(2 * BLOCK)[None, :]
    dist = BLOCK + qi - ki
    valid = (dist >= 0) & (dist <= BLOCK)
    bias = -(np.float32(slope) * np.float32(dist_scale)) * dist.astype(np.float32)
    return np.where(valid, bias, np.float32(NEG_INF)).astype(np.float32)


def _tables_a():
    return np.stack([_band_table(SLOPES_A[g * REP_A + p], 1) for p in range(REP_A) for g in range(N_KV_A)])


def _tables_b():
    return np.stack([np.stack([_band_table(SLOPES_B[g * HEADS_B + h], B_PAIRS[g][1]) for h in range(HEADS_B)])
                     for g in range(N_GROUPS_B)])


def _cparams(sem):
    return pltpu.CompilerParams(dimension_semantics=sem, vmem_limit_bytes=VMEM_LIMIT)


def _const_spec(shape):
    nd = len(shape)
    return pl.BlockSpec(shape, lambda *_: (0,) * nd, pipeline_mode=pl.Buffered(1))


def _rms_h(x_ref, nw_ref):
    x = x_ref[...]
    ms = jnp.mean(x * x, axis=-1, keepdims=True)
    return (x * lax.rsqrt(ms + RMS_EPS) * nw_ref[...]).astype(BF16)


def _inproj_prompt_body(x_ref, nw_ref, w_ref, qkva_ref, qkvb1_ref, qkvb2_ref, qkvb3_ref, gates_ref, uc_ref,
                        sa_ref, sb1_ref, sb2_ref, sb3_ref, sc_ref, zscr, *, tm):
    h = _rms_h(x_ref, nw_ref)

    def seg(lo, hi):
        return jnp.dot(h, w_ref[:, lo:hi], preferred_element_type=F32)

    za = seg(SEG_A, SEG_A + SEG_W)
    qkva_ref[...] = za.astype(BF16)
    sa_ref[0, 0] = za[tm - WINDOW_A:, WIDTH_A:WIDTH_A + WIDTH_KV_A]
    sa_ref[0, 1] = za[tm - WINDOW_A:, WIDTH_A + WIDTH_KV_A:]

    zb1 = seg(SEG_B[0], SEG_B[0] + SEG_W)
    qkvb1_ref[...] = zb1.astype(BF16)
    sb1_ref[0, 0] = zb1[tm - B_PAIRS[0][0]:, WIDTH_B:2 * WIDTH_B]
    sb1_ref[0, 1] = zb1[tm - B_PAIRS[0][0]:, 2 * WIDTH_B:]

    for g, (s_ref, q_ref) in ((1, (sb2_ref, qkvb2_ref)), (2, (sb3_ref, qkvb3_ref))):
        dil = B_PAIRS[g][1]
        zb = seg(SEG_B[g], SEG_B[g] + SEG_W)
        s_ref[0, 0] = zb[:, WIDTH_B:2 * WIDTH_B]
        s_ref[0, 1] = zb[:, 2 * WIDTH_B:]
        for c in range(SEG_W // LANES):
            zscr[c] = zb[:, c * LANES:(c + 1) * LANES]
        for r in range(dil):
            for c in range(SEG_W // LANES):
                q_ref[0, r, :, c * LANES:(c + 1) * LANES] = zscr[c, pl.ds(r, tm // dil, stride=dil), :].astype(BF16)

    gchunk = 1024
    for c in range(GATE_W // gchunk):
        gates_ref[:, c * gchunk:(c + 1) * gchunk] = seg(SEG_G + c * gchunk, SEG_G + (c + 1) * gchunk)
    zu = seg(SEG_U, SEG_U + WIDTH_C)
    uc_ref[...] = zu
    sc_ref[0] = zu[tm - POOL_STATE:, :]


def _inproj_prompt(x2d, nw, w, n_batch, seq, tm=256):
    nt = seq // tm
    m = n_batch * seq
    row = lambda n, i: (n * nt + i, 0)
    b2_tiles = B_PAIRS[1][0] // tm
    out_shape = (
        jax.ShapeDtypeStruct((m, SEG_W), BF16),
        jax.ShapeDtypeStruct((m, SEG_W), BF16),
        jax.ShapeDtypeStruct((n_batch, 4, seq // 4, SEG_W), BF16),
        jax.ShapeDtypeStruct((n_batch, 16, seq // 16, SEG_W), BF16),
        jax.ShapeDtypeStruct((m, GATE_W), F32),
        jax.ShapeDtypeStruct((m, WIDTH_C), F32),
        jax.ShapeDtypeStruct((n_batch, 2, WINDOW_A, WIDTH_KV_A), F32),
        jax.ShapeDtypeStruct((n_batch, 2, B_PAIRS[0][0], WIDTH_B), F32),
        jax.ShapeDtypeStruct((n_batch, 2, B_PAIRS[1][0], WIDTH_B), F32),
        jax.ShapeDtypeStruct((n_batch, 2, B_PAIRS[2][0], WIDTH_B), F32),
        jax.ShapeDtypeStruct((n_batch, POOL_STATE, WIDTH_C), F32),
    )
    out_specs = (
        pl.BlockSpec((tm, SEG_W), row),
        pl.BlockSpec((tm, SEG_W), row),
        pl.BlockSpec((1, 4, tm // 4, SEG_W), lambda n, i: (n, 0, i, 0)),
        pl.BlockSpec((1, 16, tm // 16, SEG_W), lambda n, i: (n, 0, i, 0)),
        pl.BlockSpec((tm, GATE_W), row),
        pl.BlockSpec((tm, WIDTH_C), row),
        pl.BlockSpec((1, 2, WINDOW_A, WIDTH_KV_A), lambda n, i: (n, 0, 0, 0)),
        pl.BlockSpec((1, 2, B_PAIRS[0][0], WIDTH_B), lambda n, i: (n, 0, 0, 0)),
        pl.BlockSpec((1, 2, tm, WIDTH_B), lambda n, i: (n, 0, jnp.maximum(i - (nt - b2_tiles), 0), 0)),
        pl.BlockSpec((1, 2, tm, WIDTH_B), lambda n, i: (n, 0, i, 0)),
        pl.BlockSpec((1, POOL_STATE, WIDTH_C), lambda n, i: (n, 0, 0)),
    )
    return pl.pallas_call(
        functools.partial(_inproj_prompt_body, tm=tm),
        grid=(n_batch, nt),
        in_specs=[pl.BlockSpec((tm, D_MODEL), row), _const_spec((1, D_MODEL)), _const_spec((D_MODEL, D_IN))],
        out_specs=out_specs,
        out_shape=out_shape,
        scratch_shapes=[pltpu.VMEM((SEG_W // LANES, tm, LANES), F32)],
        compiler_params=_cparams(("arbitrary", "arbitrary")),
        name="inproj_prompt",
    )(x2d, nw, w)


def _inproj_sample_body(x_ref, nw_ref, w_ref, qkv_ref, gates_ref, uc_ref):
    h = _rms_h(x_ref, nw_ref)
    chunk = 1024
    for c in range(QKV_W // chunk):
        qkv_ref[:, c * chunk:(c + 1) * chunk] = jnp.dot(h, w_ref[:, c * chunk:(c + 1) * chunk], preferred_element_type=F32)
    for c in range(GATE_W // chunk):
        gates_ref[:, c * chunk:(c + 1) * chunk] = jnp.dot(
            h, w_ref[:, SEG_G + c * chunk:SEG_G + (c + 1) * chunk], preferred_element_type=F32)
    uc_ref[...] = jnp.dot(h, w_ref[:, SEG_U:SEG_U + WIDTH_C], preferred_element_type=F32)


def _inproj_sample(x2d, nw, w, tm=256):
    m = x2d.shape[0]
    row = lambda i: (i, 0)
    return pl.pallas_call(
        _inproj_sample_body,
        grid=(m // tm,),
        in_specs=[pl.BlockSpec((tm, D_MODEL), row), _const_spec((1, D_MODEL)), _const_spec((D_MODEL, D_IN))],
        out_specs=(pl.BlockSpec((tm, QKV_W), row), pl.BlockSpec((tm, GATE_W), row), pl.BlockSpec((tm, WIDTH_C), row)),
        out_shape=(jax.ShapeDtypeStruct((m, QKV_W), F32), jax.ShapeDtypeStruct((m, GATE_W), F32),
                   jax.ShapeDtypeStruct((m, WIDTH_C), F32)),
        compiler_params=_cparams(("arbitrary",)),
        name="inproj_sample",
    )(x2d, nw, w)


def _lane_masks():
    lane = lax.broadcasted_iota(jnp.int32, (1, LANES), 1)
    lo = lane < HEAD_DIM
    mlo = jnp.where(lo, 1.0, 0.0).astype(BF16)
    mhi = jnp.where(lo, 0.0, 1.0).astype(BF16)
    return lo, mlo, mhi


def _attend(q, kwin, vwin, table, sink=None):
    s = lax.dot_general(q, kwin, (((1,), (1,)), ((), ())), preferred_element_type=F32) + table
    m = jnp.max(s, axis=-1, keepdims=True)
    if sink is not None:
        m = jnp.maximum(m, sink)
    e = jnp.exp(s - m)
    l = jnp.sum(e, axis=-1, keepdims=True)
    if sink is not None:
        l = l + jnp.exp(sink - m)
    acc = jnp.dot(e.astype(BF16), vwin, preferred_element_type=F32)
    return acc * (1.0 / l), m + jnp.log(l)


def _pool_diff(e_ref, rows, t0, pos0):
    cur = e_ref[pl.ds(t0, rows), :]
    acc = cur
    sums = {}
    for k in range(1, POOL_WINDOWS[-1]):
        acc = acc + e_ref[pl.ds(t0 - k, rows), :]
        if k + 1 in POOL_WINDOWS:
            sums[k + 1] = acc
    lane = lax.broadcasted_iota(jnp.int32, (rows, WIDTH_C), 1)
    pos = lax.broadcasted_iota(jnp.int32, (rows, WIDTH_C), 0) + pos0
    wsum = sums[POOL_WINDOWS[-1]]
    win = jnp.full((rows, WIDTH_C), POOL_WINDOWS[-1], jnp.int32)
    for gi in range(len(POOL_WINDOWS) - 2, -1, -1):
        sel = lane < (gi + 1) * HEAD_DIM
        wsum = jnp.where(sel, sums[POOL_WINDOWS[gi]], wsum)
        win = jnp.where(sel, POOL_WINDOWS[gi], win)
    cnt = jnp.minimum(pos + 1, win).astype(F32)
    return wsum / cnt - cur


def _attn_a_body(sink_ref, qkv_ref, uc_ref, tbl_ref, oa_ref, diff_ref, e_ref, *, seq):
    lo, mlo, mhi = _lane_masks()
    kc = WIDTH_A
    vc = WIDTH_A + WIDTH_KV_A

    def block(r0, w0, klen, table):
        kwin = qkv_ref[0, pl.ds(w0, klen), kc:kc + WIDTH_KV_A]
        vwin = qkv_ref[0, pl.ds(w0, klen), vc:vc + WIDTH_KV_A]
        for p in range(REP_A):
            qp = qkv_ref[0, pl.ds(r0, BLOCK), p * LANES:(p + 1) * LANES]
            o_lo, _ = _attend(qp * mlo, kwin, vwin, table(2 * p), sink_ref[2 * p])
            o_hi, _ = _attend(qp * mhi, kwin, vwin, table(2 * p + 1), sink_ref[2 * p + 1])
            oa_ref[0, pl.ds(r0, BLOCK), p * LANES:(p + 1) * LANES] = jnp.where(lo, o_lo, o_hi)

    block(0, 0, BLOCK, lambda u: tbl_ref[u, :, BLOCK:])

    def loop(i, c):
        r0 = pl.multiple_of(i * BLOCK, BLOCK)
        block(r0, r0 - BLOCK, 2 * BLOCK, lambda u: tbl_ref[u])
        return c

    lax.fori_loop(1, seq // BLOCK, loop, 0)

    pad = 2 * SUBLANES
    e_ref[0:pad, :] = jnp.zeros((pad, WIDTH_C), F32)
    e_ref[pad:, :] = uc_ref[0]
    rows = 256
    for c in range(seq // rows):
        diff_ref[0, c * rows:(c + 1) * rows, :] = _pool_diff(e_ref, rows, pad + c * rows, c * rows)


def _attn_a(sink, qkva, uc, tbl, n_batch, seq):
    return pl.pallas_call(
        functools.partial(_attn_a_body, seq=seq),
        grid=(n_batch,),
        in_specs=[pl.BlockSpec(memory_space=pltpu.SMEM),
                  pl.BlockSpec((1, seq, SEG_W), lambda n: (n, 0, 0)),
                  pl.BlockSpec((1, seq, WIDTH_C), lambda n: (n, 0, 0)),
                  _const_spec((N_HEADS_A, BLOCK, 2 * BLOCK))],
        out_specs=(pl.BlockSpec((1, seq, WIDTH_A), lambda n: (n, 0, 0)),
                   pl.BlockSpec((1, seq, WIDTH_C), lambda n: (n, 0, 0))),
        out_shape=(jax.ShapeDtypeStruct((n_batch, seq, WIDTH_A), F32),
                   jax.ShapeDtypeStruct((n_batch, seq, WIDTH_C), F32)),
        scratch_shapes=[pltpu.VMEM((seq + 2 * SUBLANES, WIDTH_C), F32)],
        compiler_params=_cparams(("arbitrary",)),
        name="attn_a_pool",
    )(sink, qkva, uc, tbl)


def _attn_b_body(q1_ref, q2_ref, q3_ref, tbl_ref, ob_ref, o_nat, l_nat, o_stage, l_stage, *, seq):
    lo, mlo, mhi = _lane_masks()
    npair = WIDTH_B // LANES

    def block(src, r0, w0, klen, table, dst_o, dst_l, d0):
        for p in range(npair):
            qp = src(pl.ds(r0, BLOCK), slice(p * LANES, (p + 1) * LANES))
            kwin = src(pl.ds(w0, klen), slice(WIDTH_B + p * LANES, WIDTH_B + (p + 1) * LANES))
            vwin = src(pl.ds(w0, klen), slice(2 * WIDTH_B + p * LANES, 2 * WIDTH_B + (p + 1) * LANES))
            o_lo, l_lo = _attend(qp * mlo, kwin, vwin, table(2 * p))
            o_hi, l_hi = _attend(qp * mhi, kwin, vwin, table(2 * p + 1))
            dst_o[p, pl.ds(d0, BLOCK), :] = jnp.where(lo, o_lo, o_hi)
            dst_l[p, pl.ds(d0, BLOCK), :] = jnp.where(lo, l_lo, l_hi)

    first = lambda g: (lambda u: tbl_ref[g, u, :, BLOCK:])
    full = lambda g: (lambda u: tbl_ref[g, u])

    src1 = lambda rows, cols: q1_ref[0, rows, cols]
    block(src1, 0, 0, BLOCK, first(0), o_nat.at[0], l_nat.at[0], 0)

    def loop1(i, c):
        r0 = pl.multiple_of(i * BLOCK, BLOCK)
        block(src1, r0, r0 - BLOCK, 2 * BLOCK, full(0), o_nat.at[0], l_nat.at[0], r0)
        return c

    lax.fori_loop(1, seq // BLOCK, loop1, 0)

    for g, q_ref in ((1, q2_ref), (2, q3_ref)):
        dil = B_PAIRS[g][1]
        sub = seq // dil

        def per_res(r, c, q_ref=q_ref, g=g, sub=sub):
            src = lambda rows, cols: q_ref[0, r, rows, cols]
            base = pl.multiple_of(r * sub, BLOCK)
            block(src, 0, 0, BLOCK, first(g), o_stage, l_stage, base)

            def inner(i, cc):
                r0 = pl.multiple_of(i * BLOCK, BLOCK)
                block(src, r0, r0 - BLOCK, 2 * BLOCK, full(g), o_stage, l_stage, pl.multiple_of(base + r0, BLOCK))
                return cc

            lax.fori_loop(1, sub // BLOCK, inner, 0)
            return c

        lax.fori_loop(0, dil, per_res, 0)
        for r in range(dil):
            for p in range(npair):
                o_nat[g, p, pl.ds(r, sub, stride=dil), :] = o_stage[p, r * sub:(r + 1) * sub, :]
                l_nat[g, p, pl.ds(r, sub, stride=dil), :] = l_stage[p, r * sub:(r + 1) * sub, :]

    rows = 256
    for c in range(seq // rows):
        for p in range(npair):
            ls = [l_nat[g, p, c * rows:(c + 1) * rows, :] for g in range(N_GROUPS_B)]
            mx = jnp.maximum(jnp.maximum(ls[0], ls[1]), ls[2])
            ws = [jnp.exp(x - mx) for x in ls]
            den = ws[0] + ws[1] + ws[2]
            num = sum(ws[g] * o_nat[g, p, c * rows:(c + 1) * rows, :] for g in range(N_GROUPS_B))
            ob_ref[0, c * rows:(c + 1) * rows, p * LANES:(p + 1) * LANES] = num * (1.0 / den)


def _attn_b(q1, q2, q3, tbl, n_batch, seq):
    npair = WIDTH_B // LANES
    return pl.pallas_call(
        functools.partial(_attn_b_body, seq=seq),
        grid=(n_batch,),
        in_specs=[pl.BlockSpec((1, seq, SEG_W), lambda n: (n, 0, 0)),
                  pl.BlockSpec((1, 4, seq // 4, SEG_W), lambda n: (n, 0, 0, 0)),
                  pl.BlockSpec((1, 16, seq // 16, SEG_W), lambda n: (n, 0, 0, 0)),
                  _const_spec((N_GROUPS_B, HEADS_B, BLOCK, 2 * BLOCK))],
        out_specs=pl.BlockSpec((1, seq, WIDTH_B), lambda n: (n, 0, 0)),
        out_shape=jax.ShapeDtypeStruct((n_batch, seq, WIDTH_B), F32),
        scratch_shapes=[pltpu.VMEM((N_GROUPS_B, npair, seq, LANES), F32),
                        pltpu.VMEM((N_GROUPS_B, npair, seq, LANES), F32),
                        pltpu.VMEM((npair, seq, LANES), F32),
                        pltpu.VMEM((npair, seq, LANES), F32)],
        compiler_params=_cparams(("arbitrary",)),
        name="attn_b",
    )(q1, q2, q3, tbl)


def _sigmoid(x):
    return 1.0 / (1.0 + jnp.exp(-x))


def _out_body(x_ref, oa_ref, ob_ref, diff_ref, g_ref, wpa_ref, wpb_ref, wpc_ref, wpool_ref, ps_ref, wout_ref,
              fnw_ref, y_ref, *, final):
    def gate(lo, hi):
        return g_ref[:, lo:hi]

    def silu(g):
        return g * _sigmoid(g)

    o = WIDTH_A
    ga, gb, gc = gate(0, o), gate(o, o + WIDTH_B), gate(o + WIDTH_B, o + WIDTH_B + WIDTH_C)
    mbase = o + WIDTH_B + WIDTH_C
    pa = jnp.dot((oa_ref[...] * silu(ga)).astype(BF16), wpa_ref[...], preferred_element_type=F32)
    pb = jnp.dot((ob_ref[...] * silu(gb)).astype(BF16), wpb_ref[...], preferred_element_type=F32)
    oc = jnp.dot(diff_ref[...].astype(BF16), wpool_ref[...], preferred_element_type=F32) * ps_ref[...]
    pc = jnp.dot((oc * silu(gc)).astype(BF16), wpc_ref[...], preferred_element_type=F32)
    m = (_sigmoid(gate(mbase, mbase + D_MODEL)) * pa
         + _sigmoid(gate(mbase + D_MODEL, mbase + 2 * D_MODEL)) * pb
         + _sigmoid(gate(mbase + 2 * D_MODEL, mbase + 3 * D_MODEL)) * pc)
    y = x_ref[...] + jnp.dot(m.astype(BF16), wout_ref[...], preferred_element_type=F32)
    if final:
        ms = jnp.mean(y * y, axis=-1, keepdims=True)
        y = y * lax.rsqrt(ms + RMS_EPS) * fnw_ref[...]
    y_ref[...] = y


def _out_stage(x2d, oa, ob, diff, gates, wpa, wpb, wpc, wpool, ps, wout, fnw, final, tm=256):
    m = x2d.shape[0]
    row = lambda i: (i, 0)
    return pl.pallas_call(
        functools.partial(_out_body, final=final),
        grid=(m // tm,),
        in_specs=[pl.BlockSpec((tm, D_MODEL), row), pl.BlockSpec((tm, WIDTH_A), row), pl.BlockSpec((tm, WIDTH_B), row),
                  pl.BlockSpec((tm, WIDTH_C), row), pl.BlockSpec((tm, GATE_W), row),
                  _const_spec((WIDTH_A, D_MODEL)), _const_spec((WIDTH_B, D_MODEL)), _const_spec((WIDTH_C, D_MODEL)),
                  _const_spec((WIDTH_C, WIDTH_C)), _const_spec((1, WIDTH_C)), _const_spec((D_MODEL, D_MODEL)),
                  _const_spec((1, D_MODEL))],
        out_specs=pl.BlockSpec((tm, D_MODEL), row),
        out_shape=jax.ShapeDtypeStruct((m, D_MODEL), F32),
        compiler_params=_cparams(("arbitrary",)),
        name="out_stage",
    )(x2d, oa, ob, diff, gates, wpa, wpb, wpc, wpool, ps, wout, fnw)


def _shift_rows(src, dst, kv, width, chunk=512):
    dec = 4
    main = width - SUBLANES
    for c0 in range(0, main, chunk):
        n = min(chunk, main - c0)
        dst[0, 0, kv, c0:c0 + n, :] = src[0, 0, kv, pl.ds(c0 + dec, n), :]
    tail = src[0, 0, kv, main:, :]
    row = lax.broadcasted_iota(jnp.int32, tail.shape, 0)
    dst[0, 0, kv, main:, :] = jnp.where(row < dec, pltpu.roll(tail, dec, axis=0), 0.0)


def _shift_body(a_ref, b1_ref, b2_ref, b3_ref, na_ref, nb1_ref, nb2_ref, nb3_ref):
    for src, dst in ((a_ref, na_ref), (b1_ref, nb1_ref), (b2_ref, nb2_ref), (b3_ref, nb3_ref)):
        for kv in range(2):
            _shift_rows(src, dst, kv, src.shape[3])


def _shift_states(sa, sb1, sb2, sb3):
    arrs = (sa, sb1, sb2, sb3)
    depth, nseq = sa.shape[:2]
    specs = [pl.BlockSpec((1, 1) + a.shape[2:], lambda l, n: (l, n, 0, 0, 0)) for a in arrs]
    return pl.pallas_call(
        _shift_body,
        grid=(depth, nseq),
        in_specs=specs,
        out_specs=tuple(specs),
        out_shape=tuple(jax.ShapeDtypeStruct(a.shape, a.dtype) for a in arrs),
        compiler_params=_cparams(("arbitrary", "arbitrary")),
        name="shift_states",
    )(*arrs)


def _decode_body(qkv_ref, uc_ref, sink_ref, sa_ref, sb1_ref, sb2_ref, sb3_ref, sc_ref,
                 ta_ref, tb1_ref, tb2_ref, tb3_ref,
                 oa_ref, ob_ref, diff_ref, nc_ref, na_ref, nb1_ref, nb2_ref, nb3_ref, e_ref):
    dec = 4
    row = lax.broadcasted_iota(jnp.int32, (SUBLANES, LANES), 0)
    lane = lax.broadcasted_iota(jnp.int32, (SUBLANES, LANES), 1)
    low_row = row < dec
    sel = low_row == (lane < HEAD_DIM)
    jq = jnp.where(low_row, row, row - dec)
    jq_f = jq.astype(F32)
    key_f = lane.astype(F32)

    def fold(r8):
        return jnp.where(sel, r8, pltpu.roll(r8, dec, axis=0))

    def slope_col(s_lo, s_hi):
        return jnp.where(low_row, jnp.float32(s_lo), jnp.float32(s_hi))

    def softmax_parts(s_list, extra=None):
        m = None
        for s in s_list:
            sm = jnp.max(s, axis=-1, keepdims=True)
            m = sm if m is None else jnp.maximum(m, sm)
        if extra is not None:
            m = jnp.maximum(m, extra)
        return m

    def window1(q8, ks, vs, knew, vnew, slopes, sink):
        s_st = lax.dot_general(q8.astype(BF16), ks.astype(BF16), (((1,), (1,)), ((), ())), preferred_element_type=F32)
        s_st = jnp.where(key_f >= jq_f, s_st - slopes * (float(BLOCK) + jq_f - key_f), NEG_INF)
        s_new = []
        for j2 in range(dec):
            kb = jnp.broadcast_to(knew[j2:j2 + 1, :], (SUBLANES, LANES))
            sj = jnp.sum(q8 * kb, axis=-1, keepdims=True)
            dist = jq_f[:, :1] - float(j2)
            s_new.append(jnp.where(dist >= 0.0, sj - slopes[:, :1] * dist, NEG_INF))
        m = softmax_parts([s_st] + s_new, sink)
        e_st = jnp.exp(s_st - m)
        l = jnp.sum(e_st, axis=-1, keepdims=True)
        acc = jnp.dot(e_st.astype(BF16), vs.astype(BF16), preferred_element_type=F32)
        for j2 in range(dec):
            ej = jnp.exp(s_new[j2] - m)
            l = l + ej
            acc = acc + ej * jnp.broadcast_to(vnew[j2:j2 + 1, :], (SUBLANES, LANES))
        if sink is not None:
            l = l + jnp.exp(sink - m)
        return acc * (1.0 / l), m + jnp.log(l)

    def window_d(q8, s_ref, j, c0, knew, vnew, slopes, dil):
        kj = s_ref[0, 0, 0, :, pl.ds(j * WIDTH_B + c0, LANES)]
        vj = s_ref[0, 0, 1, :, pl.ds(j * WIDTH_B + c0, LANES)]
        s = lax.dot_general(q8.astype(BF16), kj.astype(BF16), (((1,), (1,)), ((), ())), preferred_element_type=F32)
        s = s - (slopes * float(dil)) * (float(BLOCK) - key_f)
        s_self = jnp.sum(q8 * knew, axis=-1, keepdims=True)
        m = jnp.maximum(jnp.max(s, axis=-1, keepdims=True), s_self)
        e = jnp.exp(s - m)
        e_self = jnp.exp(s_self - m)
        l = jnp.sum(e, axis=-1, keepdims=True) + e_self
        acc = jnp.dot(e.astype(BF16), vj.astype(BF16), preferred_element_type=F32) + e_self * vnew
        return acc * (1.0 / l), m + jnp.log(l)

    ks, vs = sa_ref[0, 0, 0], sa_ref[0, 0, 1]
    knew = qkv_ref[0, :, WIDTH_A:WIDTH_A + WIDTH_KV_A]
    vnew = qkv_ref[0, :, WIDTH_A + WIDTH_KV_A:SEG_W]
    for p in range(REP_A):
        q8 = jnp.where(sel, qkv_ref[0, :, p * LANES:(p + 1) * LANES], 0.0)
        r8, _ = window1(q8, ks, vs, knew, vnew, slope_col(SLOPES_A[p], SLOPES_A[REP_A + p]), sink_ref[p])
        oa_ref[0, :, p * LANES:(p + 1) * LANES] = fold(r8)
    na_ref[0, 0, 0] = jnp.where(low_row, ta_ref[0, 0, 0], knew)
    na_ref[0, 0, 1] = jnp.where(low_row, ta_ref[0, 0, 1], vnew)

    for p in range(WIDTH_B // LANES):
        c0 = p * LANES
        outs, lses = [], []
        for g, s_ref in enumerate((sb1_ref, sb2_ref, sb3_ref)):
            base = SEG_B[g]
            dil = B_PAIRS[g][1]
            q8 = jnp.where(sel, qkv_ref[0, :, base + c0:base + c0 + LANES], 0.0)
            knew = qkv_ref[0, :, base + WIDTH_B + c0:base + WIDTH_B + c0 + LANES]
            vnew = qkv_ref[0, :, base + 2 * WIDTH_B + c0:base + 2 * WIDTH_B + c0 + LANES]
            slopes = slope_col(SLOPES_B[g * HEADS_B + 2 * p], SLOPES_B[g * HEADS_B + 2 * p + 1])
            if dil == 1:
                r8, lse = window1(q8, s_ref[0, 0, 0, :, c0:c0 + LANES], s_ref[0, 0, 1, :, c0:c0 + LANES],
                                  knew, vnew, slopes, None)
            else:
                r8 = jnp.zeros((SUBLANES, LANES), F32)
                lse = jnp.zeros((SUBLANES, 1), F32)
                for j in range(dec):
                    rj, lj = window_d(q8, s_ref, j, c0, knew, vnew, slopes, dil)
                    r8 = jnp.where(jq == j, rj, r8)
                    lse = jnp.where(jq[:, :1] == j, lj, lse)
            outs.append(r8)
            lses.append(lse)
        mx = jnp.maximum(jnp.maximum(lses[0], lses[1]), lses[2])
        ws = [jnp.exp(x - mx) for x in lses]
        den = ws[0] + ws[1] + ws[2]
        num = ws[0] * outs[0] + ws[1] * outs[1] + ws[2] * outs[2]
        ob_ref[0, :, c0:c0 + LANES] = fold(num * (1.0 / den))
    row_b = lax.broadcasted_iota(jnp.int32, (SUBLANES, WIDTH_B), 0) < dec
    for g, (t_ref, n_ref) in enumerate(((tb1_ref, nb1_ref), (tb2_ref, nb2_ref), (tb3_ref, nb3_ref))):
        base = SEG_B[g]
        n_ref[0, 0, 0] = jnp.where(row_b, t_ref[0, 0, 0], qkv_ref[0, :, base + WIDTH_B:base + 2 * WIDTH_B])
        n_ref[0, 0, 1] = jnp.where(row_b, t_ref[0, 0, 1], qkv_ref[0, :, base + 2 * WIDTH_B:base + 3 * WIDTH_B])

    pad = 2 * SUBLANES
    e_ref[0:1, :] = jnp.zeros((1, WIDTH_C), F32)
    e_ref[1:pad, :] = sc_ref[0, 0]
    e_ref[pad:, :] = uc_ref[0]
    d = _pool_diff(e_ref, SUBLANES, pad, POOL_WINDOWS[-1])
    row_c = lax.broadcasted_iota(jnp.int32, (SUBLANES, WIDTH_C), 0) < dec
    diff_ref[0] = jnp.where(row_c, d, pltpu.roll(d, dec, axis=0))
    nc_ref[0] = e_ref[pl.ds(pad + dec - POOL_STATE, POOL_STATE), :]


def _decode(layer, qkv8, uc8, sink8, sa, sb1, sb2v, sb3v, sc, na, nb1, nb2, nb3):
    nseq = qkv8.shape[0]
    st = lambda shape: pl.BlockSpec((1, 1) + shape, lambda n: (layer, n, 0, 0, 0))
    tail = lambda width, lanes: pl.BlockSpec((1, 1, 2, SUBLANES, lanes), lambda n: (layer, n, 0, width // SUBLANES - 1, 0))
    tails = [tail(WINDOW_A, WIDTH_KV_A), tail(B_PAIRS[0][0], WIDTH_B), tail(B_PAIRS[1][0], WIDTH_B), tail(B_PAIRS[2][0], WIDTH_B)]
    per_seq = lambda w: pl.BlockSpec((1, SUBLANES, w), lambda n: (n, 0, 0))
    in_specs = [per_seq(QKV_W), per_seq(WIDTH_C), _const_spec((REP_A, SUBLANES, 1)),
                st((2, WINDOW_A, WIDTH_KV_A)), st((2, B_PAIRS[0][0], WIDTH_B)),
                st((2, BLOCK, 4 * WIDTH_B)), st((2, BLOCK, 4 * WIDTH_B)),
                pl.BlockSpec((1, 1, POOL_STATE, WIDTH_C), lambda n: (layer, n, 0, 0))] + tails
    out_specs = (per_seq(WIDTH_A), per_seq(WIDTH_B), per_seq(WIDTH_C),
                 pl.BlockSpec((1, POOL_STATE, WIDTH_C), lambda n: (n, 0, 0))) + tuple(tails)
    out_shape = (jax.ShapeDtypeStruct((nseq, SUBLANES, WIDTH_A), F32), jax.ShapeDtypeStruct((nseq, SUBLANES, WIDTH_B), F32),
                 jax.ShapeDtypeStruct((nseq, SUBLANES, WIDTH_C), F32), jax.ShapeDtypeStruct((nseq, POOL_STATE, WIDTH_C), F32),
                 jax.ShapeDtypeStruct(na.shape, F32), jax.ShapeDtypeStruct(nb1.shape, F32),
                 jax.ShapeDtypeStruct(nb2.shape, F32), jax.ShapeDtypeStruct(nb3.shape, F32))
    return pl.pallas_call(
        _decode_body,
        grid=(nseq,),
        in_specs=in_specs,
        out_specs=out_specs,
        out_shape=out_shape,
        scratch_shapes=[pltpu.VMEM((3 * SUBLANES, WIDTH_C), F32)],
        input_output_aliases={8: 4, 9: 5, 10: 6, 11: 7},
        compiler_params=_cparams(("arbitrary",)),
        name="decode_attn",
    )(qkv8, uc8, sink8, sa, sb1, sb2v, sb3v, sc, na, nb1, nb2, nb3)


def _prep_weights(w_in, w_proj_a, w_pool):
    runs = _orig_segments()
    cols = [w_in[:, :, lo:hi] * s if s != 1.0 else w_in[:, :, lo:hi] for lo, hi, s in runs]
    w_in_p = jnp.concatenate(cols, axis=-1).astype(BF16)
    rows = [w_proj_a[:, (g * REP_A + p) * HEAD_DIM:(g * REP_A + p + 1) * HEAD_DIM, :]
            for p in range(REP_A) for g in range(N_KV_A)]
    wpa_p = jnp.concatenate(rows, axis=1).astype(BF16)
    depth = w_pool.shape[0]
    ng = len(POOL_WINDOWS)
    eye = jnp.eye(ng, dtype=w_pool.dtype)
    wpool_bd = (w_pool[:, :, :, None, :] * eye[None, :, None, :, None]).reshape(depth, WIDTH_C, WIDTH_C).astype(BF16)
    return w_in_p, wpa_p, wpool_bd


def kernel(x_prompt, x_sample, state_a, state_b1, state_b2, state_b3, state_c, norm_w, final_norm_w, w_in, sink_a,
           w_proj_a, w_proj_b, w_proj_c, w_pool, pool_scale, w_out):
    n_batch, seq, _ = x_prompt.shape
    nseq, dec, _ = x_sample.shape
    depth = w_in.shape[0]

    w_in_p, wpa_p, wpool_bd = _prep_weights(w_in, w_proj_a, w_pool)
    wpb, wpc, wout = w_proj_b.astype(BF16), w_proj_c.astype(BF16), w_out.astype(BF16)
    tbl_a = jnp.asarray(_tables_a())
    tbl_b = jnp.asarray(_tables_b())
    sink_pg = sink_a.reshape(depth, N_KV_A, REP_A).transpose(0, 2, 1)
    sink_prompt = sink_pg.reshape(depth, N_HEADS_A)
    sink_dec = jnp.repeat(sink_pg, dec, axis=2)[..., None]
    fnw = final_norm_w.reshape(1, D_MODEL)

    sa5 = state_a.reshape(depth, nseq, 2, WINDOW_A, WIDTH_KV_A)
    sb1 = state_b1.reshape(depth, nseq, 2, B_PAIRS[0][0], WIDTH_B)
    sb2 = state_b2.reshape(depth, nseq, 2, B_PAIRS[1][0], WIDTH_B)
    sb3 = state_b3.reshape(depth, nseq, 2, B_PAIRS[2][0], WIDTH_B)
    sb2v = sb2.reshape(depth, nseq, 2, BLOCK, 4 * WIDTH_B)
    sb3v = sb3.reshape(depth, nseq, 2, BLOCK, 16 * WIDTH_B)
    na, nb1, nb2, nb3 = _shift_states(sa5, sb1, sb2, sb3)

    hp = x_prompt.reshape(n_batch * seq, D_MODEL)
    hs = jnp.concatenate([x_sample, x_sample], axis=1).reshape(nseq * 2 * dec, D_MODEL)
    new_p = [[] for _ in range(5)]
    new_c_s = []
    for l in range(depth):
        nw = norm_w[l].reshape(1, D_MODEL)
        ps = pool_scale[l].reshape(1, WIDTH_C)
        final = l == depth - 1
        (qkva, qkvb1, qkvb2, qkvb3, gates, uc, s_a, s_b1, s_b2, s_b3, s_c) = _inproj_prompt(hp, nw, w_in_p[l], n_batch, seq)
        oa, diff = _attn_a(sink_prompt[l], qkva.reshape(n_batch, seq, SEG_W), uc.reshape(n_batch, seq, WIDTH_C),
                           tbl_a, n_batch, seq)
        ob = _attn_b(qkvb1.reshape(n_batch, seq, SEG_W), qkvb2, qkvb3, tbl_b, n_batch, seq)
        hp = _out_stage(hp, oa.reshape(-1, WIDTH_A), ob.reshape(-1, WIDTH_B), diff.reshape(-1, WIDTH_C), gates,
                        wpa_p[l], wpb[l], wpc[l], wpool_bd[l], ps, wout[l], fnw, final)
        for k, s in enumerate((s_a, s_b1, s_b2, s_b3, s_c)):
            new_p[k].append(s)
        qkv_s, gates_s, uc_s = _inproj_sample(hs, nw, w_in_p[l])
        oa_s, ob_s, diff_s, nc, na, nb1, nb2, nb3 = _decode(
            l, qkv_s.reshape(nseq, 2 * dec, QKV_W), uc_s.reshape(nseq, 2 * dec, WIDTH_C), sink_dec[l],
            sa5, sb1, sb2v, sb3v, state_c, na, nb1, nb2, nb3)
        hs = _out_stage(hs, oa_s.reshape(-1, WIDTH_A), ob_s.reshape(-1, WIDTH_B), diff_s.reshape(-1, WIDTH_C), gates_s,
                        wpa_p[l], wpb[l], wpc[l], wpool_bd[l], ps, wout[l], fnw, final)
        new_c_s.append(nc)

    y_prompt = hp.reshape(n_batch, seq, D_MODEL)
    y_sample = hs.reshape(nseq, 2 * dec, D_MODEL)[:, :dec]
    stack = lambda xs: jnp.stack(xs, axis=0)
    return (y_prompt, y_sample,
            stack(new_p[0]).reshape(depth, n_batch, 2, WINDOW_A, N_KV_A, HEAD_DIM), na.reshape(state_a.shape),
            stack(new_p[1]).reshape(depth, n_batch, 2, B_PAIRS[0][0], HEADS_B, HEAD_DIM), nb1.reshape(state_b1.shape),
            stack(new_p[2]).reshape(depth, n_batch, 2, B_PAIRS[1][0], HEADS_B, HEAD_DIM), nb2.reshape(state_b2.shape),
            stack(new_p[3]).reshape(depth, n_batch, 2, B_PAIRS[2][0], HEADS_B, HEAD_DIM), nb3.reshape(state_b3.shape),
            stack(new_p[4]), stack(new_c_s))
```

```python
import functools

import numpy as np
import jax
import jax.numpy as jnp
from jax import lax
from jax.experimental import pallas as pl
from jax.experimental.pallas import tpu as pltpu

F32 = jnp.float32
BF16 = jnp.bfloat16

D_MODEL = 1024
HEAD_DIM = 64
N_HEADS_A = 8
N_KV_A = 2
REP_A = N_HEADS_A // N_KV_A
WINDOW_A = 128
B_PAIRS = ((128, 1), (512, 4), (2048, 16))
N_GROUPS_B = 3
HEADS_B = 4
POOL_WINDOWS = (2, 4, 8, 16)
POOL_STATE = 15
WIDTH_A = N_HEADS_A * HEAD_DIM
WIDTH_KV_A = N_KV_A * HEAD_DIM
WIDTH_B = HEADS_B * HEAD_DIM
WIDTH_C = 256
BLOCK = 128
RMS_EPS = 1e-6
NEG_INF = -1e30
LANES = 128
SUBLANES = 8
VMEM_LIMIT = 56 * 1024 * 1024
DEC = 4
DEC_ROWS = 2 * DEC

COL_QA, COL_KA, COL_VA, COL_GA = 0, 512, 640, 768
COL_QB, COL_KB, COL_VB, COL_GB = 1280, 2048, 2816, 3584
COL_UC, COL_GC, COL_MG = 3840, 4096, 4352
D_IN = 7424
SEG_W = 768
QKV_W = 4 * SEG_W
GATE_W = 4096
KVT_A = 0
KVT_B = (256, 768, 1280)
KVT_ROWS = 1792


def _alibi_slopes(n):
    return 2.0 ** (-8.0 * (np.arange(n, dtype=np.float32) + 1.0) / n)


SLOPES_A = _alibi_slopes(N_HEADS_A)
SLOPES_B = _alibi_slopes(N_GROUPS_B * HEADS_B)


WIN_PREV_CUR = 0
WIN_CUR_NEXT = 1
WIN_MASKED_CUR = 2
N_WIN_KINDS = 3


def _band_table(slope, dist_scale, kind):
    qi = np.arange(BLOCK)[:, None]
    ki = np.arange(2 * BLOCK)[None, :]
    own = (ki < BLOCK) if kind == WIN_CUR_NEXT else (ki >= BLOCK)
    dist = (qi - ki) if kind == WIN_CUR_NEXT else (BLOCK + qi - ki)
    valid = (dist >= 0) & (dist <= BLOCK)
    if kind != WIN_PREV_CUR:
        valid &= own
    bias = -(np.float32(slope) * np.float32(dist_scale)) * dist.astype(np.float32)
    return np.where(valid, bias, np.float32(NEG_INF)).astype(np.float32)


def _tables_a():
    return np.stack([np.stack([_band_table(SLOPES_A[g * REP_A + p], 1, kind) for p in range(REP_A) for g in range(N_KV_A)])
                     for kind in (WIN_PREV_CUR, WIN_CUR_NEXT)])


def _tables_b():
    return np.stack([np.stack([np.stack([_band_table(SLOPES_B[g * HEADS_B + h], B_PAIRS[g][1], kind)
                                         for h in range(HEADS_B)]) for g in range(N_GROUPS_B)])
                     for kind in range(N_WIN_KINDS)])


def _decode_slopes():
    rows_a = np.array([SLOPES_A[(r % DEC_ROWS) // DEC * REP_A + r // DEC_ROWS] for r in range(REP_A * DEC_ROWS)], np.float32)
    rows_b = np.array([[SLOPES_B[g * HEADS_B + r // DEC] for r in range(HEADS_B * DEC)] for g in range(N_GROUPS_B)], np.float32)
    return (np.broadcast_to(rows_a[:, None], (REP_A * DEC_ROWS, LANES)).copy(),
            np.broadcast_to(rows_b[:, :, None], (N_GROUPS_B, HEADS_B * DEC, LANES)).copy())


def _cparams(sem):
    return pltpu.CompilerParams(dimension_semantics=sem, vmem_limit_bytes=VMEM_LIMIT)


def _const_spec(shape):
    nd = len(shape)
    return pl.BlockSpec(shape, lambda *_: (0,) * nd, pipeline_mode=pl.Buffered(1))


def _layer_spec(layer, shape):
    nd = len(shape)
    return pl.BlockSpec((1,) + shape, lambda *_: (layer,) + (0,) * nd, pipeline_mode=pl.Buffered(1))


def _rms_h(x_ref, nw_ref):
    x = x_ref[...]
    ms = jnp.mean(x * x, axis=-1, keepdims=True)
    return (x * lax.rsqrt(ms + RMS_EPS) * nw_ref[0]).astype(BF16)


def _proj(h, w_ref, lo, width):
    return jnp.dot(h, w_ref[0, :, lo:lo + width], preferred_element_type=F32)


def _store_gates(h, w_ref, gates_ref):
    o = 0
    for lo, width in ((COL_GA, WIDTH_A), (COL_GB, WIDTH_B), (COL_GC, WIDTH_C),
                      (COL_MG, D_MODEL), (COL_MG + D_MODEL, D_MODEL), (COL_MG + 2 * D_MODEL, D_MODEL)):
        gates_ref[:, o:o + width] = _proj(h, w_ref, lo, width).astype(BF16)
        o += width


def _inproj_prompt_body(x_ref, nw_ref, w_ref, qkva_ref, qkvb1_ref, qkvb2_ref, qkvb3_ref, gates_ref, uc_ref,
                        sa_ref, sb1_ref, sb2_ref, sb3_ref, sc_ref, zscr, *, tm):
    h = _rms_h(x_ref, nw_ref)

    za = _proj(h, w_ref, COL_QA, SEG_W)
    qkva_ref[...] = za.astype(BF16)
    sa_ref[0, 0] = za[tm - WINDOW_A:, WIDTH_A:WIDTH_A + WIDTH_KV_A].T
    sa_ref[0, 1] = za[tm - WINDOW_A:, WIDTH_A + WIDTH_KV_A:].T

    for g, (s_ref, q_ref) in enumerate(((sb1_ref, qkvb1_ref), (sb2_ref, qkvb2_ref), (sb3_ref, qkvb3_ref))):
        width, dil = B_PAIRS[g]
        keep = min(width, tm)
        parts = [_proj(h, w_ref, col + g * WIDTH_B, WIDTH_B) for col in (COL_QB, COL_KB, COL_VB)]
        s_ref[0, 0] = parts[1][tm - keep:, :].T
        s_ref[0, 1] = parts[2][tm - keep:, :].T
        if dil == 1:
            for k, z in enumerate(parts):
                q_ref[:, k * WIDTH_B:(k + 1) * WIDTH_B] = z.astype(BF16)
            continue
        ncol = WIDTH_B // LANES
        for k, z in enumerate(parts):
            for c in range(ncol):
                zscr[k * ncol + c] = z[:, c * LANES:(c + 1) * LANES]
        for r in range(dil):
            for c in range(SEG_W // LANES):
                q_ref[0, r, :, c * LANES:(c + 1) * LANES] = zscr[c, pl.ds(r, tm // dil, stride=dil), :].astype(BF16)

    _store_gates(h, w_ref, gates_ref)
    zu = _proj(h, w_ref, COL_UC, WIDTH_C)
    uc_ref[...] = zu
    sc_ref[0] = zu[tm - POOL_STATE:, :]


def _inproj_prompt(layer, x2d, nw, w, n_batch, seq, tm=256):
    nt = seq // tm
    m = n_batch * seq
    row = lambda n, i: (n * nt + i, 0)
    b2_tiles = B_PAIRS[1][0] // tm
    out_shape = (
        jax.ShapeDtypeStruct((m, SEG_W), BF16),
        jax.ShapeDtypeStruct((m, SEG_W), BF16),
        jax.ShapeDtypeStruct((n_batch, 4, seq // 4, SEG_W), BF16),
        jax.ShapeDtypeStruct((n_batch, 16, seq // 16, SEG_W), BF16),
        jax.ShapeDtypeStruct((m, GATE_W), BF16),
        jax.ShapeDtypeStruct((m, WIDTH_C), F32),
        jax.ShapeDtypeStruct((n_batch, 2, WIDTH_KV_A, WINDOW_A), F32),
        jax.ShapeDtypeStruct((n_batch, 2, WIDTH_B, B_PAIRS[0][0]), F32),
        jax.ShapeDtypeStruct((n_batch, 2, WIDTH_B, B_PAIRS[1][0]), F32),
        jax.ShapeDtypeStruct((n_batch, 2, WIDTH_B, B_PAIRS[2][0]), F32),
        jax.ShapeDtypeStruct((n_batch, POOL_STATE, WIDTH_C), F32),
    )
    out_specs = (
        pl.BlockSpec((tm, SEG_W), row),
        pl.BlockSpec((tm, SEG_W), row),
        pl.BlockSpec((1, 4, tm // 4, SEG_W), lambda n, i: (n, 0, i, 0)),
        pl.BlockSpec((1, 16, tm // 16, SEG_W), lambda n, i: (n, 0, i, 0)),
        pl.BlockSpec((tm, GATE_W), row),
        pl.BlockSpec((tm, WIDTH_C), row),
        pl.BlockSpec((1, 2, WIDTH_KV_A, WINDOW_A), lambda n, i: (n, 0, 0, 0)),
        pl.BlockSpec((1, 2, WIDTH_B, B_PAIRS[0][0]), lambda n, i: (n, 0, 0, 0)),
        pl.BlockSpec((1, 2, WIDTH_B, tm), lambda n, i: (n, 0, 0, jnp.maximum(i - (nt - b2_tiles), 0))),
        pl.BlockSpec((1, 2, WIDTH_B, tm), lambda n, i: (n, 0, 0, i)),
        pl.BlockSpec((1, POOL_STATE, WIDTH_C), lambda n, i: (n, 0, 0)),
    )
    return pl.pallas_call(
        functools.partial(_inproj_prompt_body, tm=tm),
        grid=(n_batch, nt),
        in_specs=[pl.BlockSpec((tm, D_MODEL), row), _layer_spec(layer, (1, D_MODEL)), _layer_spec(layer, (D_MODEL, D_IN))],
        out_specs=out_specs,
        out_shape=out_shape,
        scratch_shapes=[pltpu.VMEM((SEG_W // LANES, tm, LANES), F32)],
        compiler_params=_cparams(("arbitrary", "arbitrary")),
        name="inproj_prompt",
    )(x2d, nw, w)


def _inproj_sample_body(x_ref, nw_ref, w_ref, qkv_ref, gates_ref, uc_ref, kvt_ref):
    h = _rms_h(x_ref, nw_ref)
    za = _proj(h, w_ref, COL_QA, SEG_W)
    qkv_ref[:, 0:SEG_W] = za
    kvt_ref[KVT_A:KVT_A + WIDTH_KV_A, :] = za[:, WIDTH_A:WIDTH_A + WIDTH_KV_A].T
    kvt_ref[KVT_A + WIDTH_KV_A:KVT_A + 2 * WIDTH_KV_A, :] = za[:, WIDTH_A + WIDTH_KV_A:].T
    for g in range(N_GROUPS_B):
        for k, col in enumerate((COL_QB, COL_KB, COL_VB)):
            o = (1 + g) * SEG_W + k * WIDTH_B
            z = _proj(h, w_ref, col + g * WIDTH_B, WIDTH_B)
            qkv_ref[:, o:o + WIDTH_B] = z
            if k > 0:
                r0 = KVT_B[g] + (k - 1) * WIDTH_B
                kvt_ref[r0:r0 + WIDTH_B, :] = z.T
    _store_gates(h, w_ref, gates_ref)
    uc_ref[...] = _proj(h, w_ref, COL_UC, WIDTH_C)


def _inproj_sample(layer, x2d, nw, w, tm=256):
    m = x2d.shape[0]
    row = lambda i: (i, 0)
    return pl.pallas_call(
        _inproj_sample_body,
        grid=(m // tm,),
        in_specs=[pl.BlockSpec((tm, D_MODEL), row), _layer_spec(layer, (1, D_MODEL)), _layer_spec(layer, (D_MODEL, D_IN))],
        out_specs=(pl.BlockSpec((tm, QKV_W), row), pl.BlockSpec((tm, GATE_W), row), pl.BlockSpec((tm, WIDTH_C), row),
                   pl.BlockSpec((KVT_ROWS, tm), lambda i: (0, i))),
        out_shape=(jax.ShapeDtypeStruct((m, QKV_W), F32), jax.ShapeDtypeStruct((m, GATE_W), BF16),
                   jax.ShapeDtypeStruct((m, WIDTH_C), F32), jax.ShapeDtypeStruct((KVT_ROWS, m), F32)),
        compiler_params=_cparams(("arbitrary",)),
        name="inproj_sample",
    )(x2d, nw, w)


def _lane_masks():
    lane = lax.broadcasted_iota(jnp.int32, (1, LANES), 1)
    lo = lane < HEAD_DIM
    mlo = jnp.where(lo, 1.0, 0.0).astype(BF16)
    mhi = jnp.where(lo, 0.0, 1.0).astype(BF16)
    return lo, mlo, mhi


def _attend(q, kwin, vwin, table, sink=None):
    s = lax.dot_general(q, kwin, (((1,), (1,)), ((), ())), preferred_element_type=F32) + table
    m = jnp.max(s, axis=-1, keepdims=True)
    if sink is not None:
        m = jnp.maximum(m, sink)
    e = jnp.exp(s - m)
    l = jnp.sum(e, axis=-1, keepdims=True)
    if sink is not None:
        l = l + jnp.exp(sink - m)
    acc = jnp.dot(e.astype(BF16), vwin, preferred_element_type=F32)
    return acc * (1.0 / l), m + jnp.log(l)


def _pool_diff(e_ref, rows, t0, pos0):
    cur = e_ref[pl.ds(t0, rows), :]
    acc = cur
    sums = {}
    for k in range(1, POOL_WINDOWS[-1]):
        acc = acc + e_ref[pl.ds(t0 - k, rows), :]
        if k + 1 in POOL_WINDOWS:
            sums[k + 1] = acc
    lane = lax.broadcasted_iota(jnp.int32, (rows, WIDTH_C), 1)
    pos = lax.broadcasted_iota(jnp.int32, (rows, WIDTH_C), 0) + pos0
    wsum = sums[POOL_WINDOWS[-1]]
    win = jnp.full((rows, WIDTH_C), POOL_WINDOWS[-1], jnp.int32)
    for gi in range(len(POOL_WINDOWS) - 2, -1, -1):
        sel = lane < (gi + 1) * HEAD_DIM
        wsum = jnp.where(sel, sums[POOL_WINDOWS[gi]], wsum)
        win = jnp.where(sel, POOL_WINDOWS[gi], win)
    cnt = jnp.minimum(pos + 1, win).astype(F32)
    return wsum / cnt - cur


def _attn_a_body(sink_ref, qkv_ref, uc_ref, tbl_ref, oa_ref, diff_ref, e_ref, *, seq):
    lo, mlo, mhi = _lane_masks()
    kc = WIDTH_A
    vc = WIDTH_A + WIDTH_KV_A

    def block(r0, w0, kind):
        kwin = qkv_ref[0, pl.ds(w0, 2 * BLOCK), kc:kc + WIDTH_KV_A]
        vwin = qkv_ref[0, pl.ds(w0, 2 * BLOCK), vc:vc + WIDTH_KV_A]
        for p in range(REP_A):
            qp = qkv_ref[0, pl.ds(r0, BLOCK), p * LANES:(p + 1) * LANES]
            o_lo, _ = _attend(qp * mlo, kwin, vwin, tbl_ref[kind, 2 * p], sink_ref[2 * p])
            o_hi, _ = _attend(qp * mhi, kwin, vwin, tbl_ref[kind, 2 * p + 1], sink_ref[2 * p + 1])
            oa_ref[0, pl.ds(r0, BLOCK), p * LANES:(p + 1) * LANES] = jnp.where(lo, o_lo, o_hi)

    block(0, 0, WIN_CUR_NEXT)

    def loop(i, c):
        r0 = pl.multiple_of(i * BLOCK, BLOCK)
        block(r0, r0 - BLOCK, WIN_PREV_CUR)
        return c

    lax.fori_loop(1, seq // BLOCK, loop, 0)

    pad = 2 * SUBLANES
    e_ref[0:pad, :] = jnp.zeros((pad, WIDTH_C), F32)
    e_ref[pad:, :] = uc_ref[0]
    rows = 256
    for c in range(seq // rows):
        diff_ref[0, c * rows:(c + 1) * rows, :] = _pool_diff(e_ref, rows, pad + c * rows, c * rows)


def _attn_a(sink, qkva, uc, tbl, n_batch, seq):
    return pl.pallas_call(
        functools.partial(_attn_a_body, seq=seq),
        grid=(n_batch,),
        in_specs=[pl.BlockSpec(memory_space=pltpu.SMEM),
                  pl.BlockSpec((1, seq, SEG_W), lambda n: (n, 0, 0)),
                  pl.BlockSpec((1, seq, WIDTH_C), lambda n: (n, 0, 0)),
                  _const_spec((2, N_HEADS_A, BLOCK, 2 * BLOCK))],
        out_specs=(pl.BlockSpec((1, seq, WIDTH_A), lambda n: (n, 0, 0)),
                   pl.BlockSpec((1, seq, WIDTH_C), lambda n: (n, 0, 0))),
        out_shape=(jax.ShapeDtypeStruct((n_batch, seq, WIDTH_A), F32),
                   jax.ShapeDtypeStruct((n_batch, seq, WIDTH_C), F32)),
        scratch_shapes=[pltpu.VMEM((seq + 2 * SUBLANES, WIDTH_C), F32)],
        compiler_params=_cparams(("arbitrary",)),
        name="attn_a_pool",
    )(sink, qkva, uc, tbl)


def _attn_b_body(q1_ref, q2_ref, q3_ref, tbl_ref, ob_ref, o_nat, l_nat, o_stage, l_stage, *, seq):
    lo, mlo, mhi = _lane_masks()
    npair = WIDTH_B // LANES

    def block(q_ref, r0, w0, table, dst_o, dst_l):
        for p in range(npair):
            qp = q_ref[0, pl.ds(r0, BLOCK), p * LANES:(p + 1) * LANES]
            kwin = q_ref[0, pl.ds(w0, 2 * BLOCK), WIDTH_B + p * LANES:WIDTH_B + (p + 1) * LANES]
            vwin = q_ref[0, pl.ds(w0, 2 * BLOCK), 2 * WIDTH_B + p * LANES:2 * WIDTH_B + (p + 1) * LANES]
            o_lo, l_lo = _attend(qp * mlo, kwin, vwin, table(2 * p))
            o_hi, l_hi = _attend(qp * mhi, kwin, vwin, table(2 * p + 1))
            dst_o[p, pl.ds(r0, BLOCK), :] = jnp.where(lo, o_lo, o_hi)
            dst_l[p, pl.ds(r0, BLOCK), :] = jnp.where(lo, l_lo, l_hi)

    table = lambda kind, g: (lambda u: tbl_ref[kind, g, u])

    block(q1_ref, 0, 0, table(WIN_CUR_NEXT, 0), o_nat.at[0], l_nat.at[0])

    def loop1(i, c):
        r0 = pl.multiple_of(i * BLOCK, BLOCK)
        block(q1_ref, r0, r0 - BLOCK, table(WIN_PREV_CUR, 0), o_nat.at[0], l_nat.at[0])
        return c

    lax.fori_loop(1, seq // BLOCK, loop1, 0, unroll=3)

    sub = seq // B_PAIRS[1][1]
    for r in range(B_PAIRS[1][1]):
        block(q2_ref, r * sub, r * sub, table(WIN_CUR_NEXT, 1), o_stage, l_stage)

    def loop2(i, c):
        for r in range(B_PAIRS[1][1]):
            r0 = pl.multiple_of(r * sub + i * BLOCK, BLOCK)
            block(q2_ref, r0, r0 - BLOCK, table(WIN_PREV_CUR, 1), o_stage, l_stage)
        return c

    lax.fori_loop(1, sub // BLOCK, loop2, 0)
    for r in range(B_PAIRS[1][1]):
        for p in range(npair):
            o_nat[1, p, pl.ds(r, sub, stride=B_PAIRS[1][1]), :] = o_stage[p, r * sub:(r + 1) * sub, :]
            l_nat[1, p, pl.ds(r, sub, stride=B_PAIRS[1][1]), :] = l_stage[p, r * sub:(r + 1) * sub, :]

    dil3 = B_PAIRS[2][1]
    par = 2

    def loop3(rg, c):
        for k in range(par):
            r = rg * par + k
            last = r // (dil3 - 1)
            r0 = pl.multiple_of(r * BLOCK, BLOCK)
            w0 = pl.multiple_of((r - last) * BLOCK, BLOCK)
            block(q3_ref, r0, w0, (lambda u, last=last: tbl_ref[WIN_CUR_NEXT + last, 2, u]), o_stage, l_stage)
        return c

    lax.fori_loop(0, dil3 // par, loop3, 0)
    for r in range(dil3):
        for p in range(npair):
            o_nat[2, p, pl.ds(r, BLOCK, stride=dil3), :] = o_stage[p, r * BLOCK:(r + 1) * BLOCK, :]
            l_nat[2, p, pl.ds(r, BLOCK, stride=dil3), :] = l_stage[p, r * BLOCK:(r + 1) * BLOCK, :]

    rows = 256
    for c in range(seq // rows):
        for p in range(npair):
            ls = [l_nat[g, p, c * rows:(c + 1) * rows, :] for g in range(N_GROUPS_B)]
            mx = jnp.maximum(jnp.maximum(ls[0], ls[1]), ls[2])
            ws = [jnp.exp(x - mx) for x in ls]
            den = ws[0] + ws[1] + ws[2]
            num = sum(ws[g] * o_nat[g, p, c * rows:(c + 1) * rows, :] for g in range(N_GROUPS_B))
            ob_ref[0, c * rows:(c + 1) * rows, p * LANES:(p + 1) * LANES] = num * (1.0 / den)


def _attn_b(q1, q2, q3, tbl, n_batch, seq):
    npair = WIDTH_B // LANES
    return pl.pallas_call(
        functools.partial(_attn_b_body, seq=seq),
        grid=(n_batch,),
        in_specs=[pl.BlockSpec((1, seq, SEG_W), lambda n: (n, 0, 0))] * N_GROUPS_B
                 + [_const_spec((N_WIN_KINDS, N_GROUPS_B, HEADS_B, BLOCK, 2 * BLOCK))],
        out_specs=pl.BlockSpec((1, seq, WIDTH_B), lambda n: (n, 0, 0)),
        out_shape=jax.ShapeDtypeStruct((n_batch, seq, WIDTH_B), F32),
        scratch_shapes=[pltpu.VMEM((N_GROUPS_B, npair, seq, LANES), F32),
                        pltpu.VMEM((N_GROUPS_B, npair, seq, LANES), F32),
                        pltpu.VMEM((npair, seq, LANES), F32),
                        pltpu.VMEM((npair, seq, LANES), F32)],
        compiler_params=_cparams(("arbitrary",)),
        name="attn_b",
    )(q1, q2, q3, tbl)


def _sigmoid(x):
    return 1.0 / (1.0 + jnp.exp(-x))


def _out_body(x_ref, oa_ref, ob_ref, diff_ref, g_ref, wpa_ref, wpb_ref, wpc_ref, wpool_ref, ps_ref, wout_ref,
              fnw_ref, y_ref, *, final):
    def gate(lo, hi):
        return g_ref[:, lo:hi].astype(F32)

    def silu(g):
        return g * _sigmoid(g)

    o = WIDTH_A
    ga, gb, gc = gate(0, o), gate(o, o + WIDTH_B), gate(o + WIDTH_B, o + WIDTH_B + WIDTH_C)
    mbase = o + WIDTH_B + WIDTH_C
    pa = jnp.dot((oa_ref[...] * silu(ga)).astype(BF16), wpa_ref[0], preferred_element_type=F32)
    pb = jnp.dot((ob_ref[...] * silu(gb)).astype(BF16), wpb_ref[0], preferred_element_type=F32)
    oc = jnp.dot(diff_ref[...].astype(BF16), wpool_ref[0], preferred_element_type=F32) * ps_ref[0]
    pc = jnp.dot((oc * silu(gc)).astype(BF16), wpc_ref[0], preferred_element_type=F32)
    m = (_sigmoid(gate(mbase, mbase + D_MODEL)) * pa
         + _sigmoid(gate(mbase + D_MODEL, mbase + 2 * D_MODEL)) * pb
         + _sigmoid(gate(mbase + 2 * D_MODEL, mbase + 3 * D_MODEL)) * pc)
    y = x_ref[...] + jnp.dot(m.astype(BF16), wout_ref[0], preferred_element_type=F32)
    if final:
        ms = jnp.mean(y * y, axis=-1, keepdims=True)
        y = y * lax.rsqrt(ms + RMS_EPS) * fnw_ref[...]
    y_ref[...] = y


def _out_stage(layer, x2d, oa, ob, diff, gates, wpa, wpb, wpc, wpool, ps, wout, fnw, final, tm=256):
    m = x2d.shape[0]
    row = lambda i: (i, 0)
    return pl.pallas_call(
        functools.partial(_out_body, final=final),
        grid=(m // tm,),
        in_specs=[pl.BlockSpec((tm, D_MODEL), row), pl.BlockSpec((tm, WIDTH_A), row), pl.BlockSpec((tm, WIDTH_B), row),
                  pl.BlockSpec((tm, WIDTH_C), row), pl.BlockSpec((tm, GATE_W), row),
                  _layer_spec(layer, (WIDTH_A, D_MODEL)), _layer_spec(layer, (WIDTH_B, D_MODEL)),
                  _layer_spec(layer, (WIDTH_C, D_MODEL)), _layer_spec(layer, (WIDTH_C, WIDTH_C)),
                  _layer_spec(layer, (1, WIDTH_C)), _layer_spec(layer, (D_MODEL, D_MODEL)),
                  _const_spec((1, D_MODEL))],
        out_specs=pl.BlockSpec((tm, D_MODEL), row),
        out_shape=jax.ShapeDtypeStruct((m, D_MODEL), F32),
        compiler_params=_cparams(("arbitrary",)),
        name="out_stage",
    )(x2d, oa, ob, diff, gates, wpa, wpb, wpc, wpool, ps, wout, fnw)


def _decode_body(*refs, has_prev):
    qkv_ref, kvt_ref, sink_ref, sla_ref, slb_ref, sa_ref, sb1_ref, sb2_ref, sb3_ref = refs[:9]
    outs = refs[9 + (4 if has_prev else 0):]
    oa_ref, ob_ref, na_ref, nb1_ref, nb2_ref, nb3_ref = outs
    n = pl.program_id(0)
    new_shift = (LANES - DEC) - (n % (LANES // DEC_ROWS)) * DEC_ROWS

    def shifted_state(s_ref, n_ref, kv, rows, width, kvt_row):
        x = s_ref[0, 0, kv].reshape(rows, width)
        rolled = pltpu.roll(x, width - DEC, axis=1)
        new_t = pltpu.roll(kvt_ref[kvt_row:kvt_row + rows, :], new_shift, axis=1)
        lane = lax.broadcasted_iota(jnp.int32, (rows, LANES), 1)
        last = jnp.where(lane < LANES - DEC, rolled[:, width - LANES:], new_t)
        heads = rows // HEAD_DIM
        if width > LANES:
            n_ref[0, 0, kv, :, :, 0:width - LANES] = rolled[:, 0:width - LANES].reshape(heads, HEAD_DIM, width - LANES)
        n_ref[0, 0, kv, :, :, width - LANES:width] = last.reshape(heads, HEAD_DIM, LANES)
        return x

    def attend(q, kt, vt, knew, vnew, live, slopes, width, dil, sink):
        rows = q.shape[0]
        r_i = lax.broadcasted_iota(jnp.int32, (rows, width), 0)
        t_i = lax.broadcasted_iota(jnp.int32, (rows, width), 1)
        jq = r_i & (DEC - 1)
        s = jnp.dot(q.astype(BF16), kt.astype(BF16), preferred_element_type=F32)
        dist = (float(width) + jq.astype(F32)) - t_i.astype(F32)
        valid = (t_i >= jq) if dil == 1 else ((t_i & (dil - 1)) == jq)
        s = jnp.where(valid, s - slopes * dist, NEG_INF)
        jq1 = jq[:, :1].astype(F32)
        s_new = []
        if dil == 1:
            for j2 in range(DEC):
                kb = jnp.broadcast_to(knew[j2:j2 + 1, :], q.shape)
                d2 = jq1 - float(j2)
                s_new.append(jnp.where(d2 >= 0.0, jnp.sum(q * kb, axis=-1, keepdims=True) - slopes * d2, NEG_INF))
        else:
            s_new.append(jnp.sum(q * knew, axis=-1, keepdims=True))
        m = jnp.max(s, axis=-1, keepdims=True)
        for sn in s_new:
            m = jnp.maximum(m, sn)
        if sink is not None:
            m = jnp.maximum(m, sink)
        e = jnp.exp(s - m)
        l = jnp.sum(e, axis=-1, keepdims=True)
        acc = lax.dot_general(e.astype(BF16), vt.astype(BF16), (((1,), (1,)), ((), ())), preferred_element_type=F32)
        for j2, sn in enumerate(s_new):
            en = jnp.exp(sn - m)
            l = l + en
            vb = jnp.broadcast_to(vnew[j2:j2 + 1, :], q.shape) if dil == 1 else vnew
            acc = acc + en * vb
        if sink is not None:
            l = l + jnp.exp(sink - m)
        return jnp.where(live, acc * (1.0 / l), 0.0), m + jnp.log(l)

    kt = shifted_state(sa_ref, na_ref, 0, WIDTH_KV_A, WINDOW_A, KVT_A)
    vt = shifted_state(sa_ref, na_ref, 1, WIDTH_KV_A, WINDOW_A, KVT_A + WIDTH_KV_A)
    ra = REP_A * DEC_ROWS
    row = lax.broadcasted_iota(jnp.int32, (ra, LANES), 0)
    lane = lax.broadcasted_iota(jnp.int32, (ra, LANES), 1)
    live_a = ((row & (DEC_ROWS - 1)) < DEC) == (lane < HEAD_DIM)
    q32 = jnp.concatenate([qkv_ref[0, :, p * LANES:(p + 1) * LANES] for p in range(REP_A)], axis=0)
    q32 = jnp.where(live_a, q32, 0.0)
    knew = jnp.concatenate([qkv_ref[0, :, WIDTH_A:WIDTH_A + WIDTH_KV_A]] * REP_A, axis=0)
    vnew = jnp.concatenate([qkv_ref[0, :, WIDTH_A + WIDTH_KV_A:SEG_W]] * REP_A, axis=0)
    o32, _ = attend(q32, kt, vt, knew, vnew, live_a, sla_ref[:, 0:1], WINDOW_A, 1, sink_ref[0])
    for p in range(REP_A):
        r8 = o32[p * DEC_ROWS:(p + 1) * DEC_ROWS]
        oa_ref[0, :, p * LANES:(p + 1) * LANES] = r8 + pltpu.roll(r8, DEC, axis=0)

    rb = HEADS_B * DEC
    row = lax.broadcasted_iota(jnp.int32, (rb, WIDTH_B), 0)
    lane = lax.broadcasted_iota(jnp.int32, (rb, WIDTH_B), 1)
    live_b = (lane // HEAD_DIM) == (row // DEC)
    o_g, lse_g = [], []
    for g, (s_ref, n_ref) in enumerate(((sb1_ref, nb1_ref), (sb2_ref, nb2_ref), (sb3_ref, nb3_ref))):
        width, dil = B_PAIRS[g]
        base = (1 + g) * SEG_W
        kt = shifted_state(s_ref, n_ref, 0, WIDTH_B, width, KVT_B[g])
        vt = shifted_state(s_ref, n_ref, 1, WIDTH_B, width, KVT_B[g] + WIDTH_B)
        q16 = jnp.where(live_b, jnp.concatenate([qkv_ref[0, :, base:base + WIDTH_B]] * 2, axis=0), 0.0)
        knew = jnp.concatenate([qkv_ref[0, :, base + WIDTH_B:base + 2 * WIDTH_B]] * 2, axis=0)
        vnew = jnp.concatenate([qkv_ref[0, :, base + 2 * WIDTH_B:base + 3 * WIDTH_B]] * 2, axis=0)
        o16, lse = attend(q16, kt, vt, knew, vnew, live_b, slb_ref[g, :, 0:1], width, dil, None)
        o_g.append(o16)
        lse_g.append(lse)
    mx = jnp.maximum(jnp.maximum(lse_g[0], lse_g[1]), lse_g[2])
    ws = [jnp.exp(x - mx) for x in lse_g]
    o16 = (ws[0] * o_g[0] + ws[1] * o_g[1] + ws[2] * o_g[2]) * (1.0 / (ws[0] + ws[1] + ws[2]))
    x8 = o16[0:DEC_ROWS] + o16[DEC_ROWS:]
    ob_ref[0] = x8 + pltpu.roll(x8, DEC, axis=0)


def _decode(layer, qkv8, kvt, sink32, slopes_a, slopes_b, states, prev):
    nseq = qkv8.shape[0]
    st = lambda a: pl.BlockSpec((1, 1) + a.shape[2:], lambda n: (layer, n, 0, 0, 0, 0))
    per_seq = lambda w: pl.BlockSpec((1, DEC_ROWS, w), lambda n: (n, 0, 0))
    in_specs = [per_seq(QKV_W),
                pl.BlockSpec((KVT_ROWS, LANES), lambda n: (0, n // (LANES // DEC_ROWS))),
                _layer_spec(layer, sink32.shape[1:]), _const_spec(slopes_a.shape), _const_spec(slopes_b.shape)]
    in_specs += [st(a) for a in states]
    args = [qkv8, kvt, sink32, slopes_a, slopes_b, *states]
    aliases = {}
    if prev is not None:
        in_specs += [pl.BlockSpec(memory_space=pl.ANY)] * len(prev)
        aliases = {len(args) + k: 2 + k for k in range(len(prev))}
        args += list(prev)
    out_specs = (per_seq(WIDTH_A), per_seq(WIDTH_B)) + tuple(st(a) for a in states)
    out_shape = (jax.ShapeDtypeStruct((nseq, DEC_ROWS, WIDTH_A), F32), jax.ShapeDtypeStruct((nseq, DEC_ROWS, WIDTH_B), F32)
                 ) + tuple(jax.ShapeDtypeStruct(a.shape, F32) for a in states)
    return pl.pallas_call(
        functools.partial(_decode_body, has_prev=prev is not None),
        grid=(nseq,),
        in_specs=in_specs,
        out_specs=out_specs,
        out_shape=out_shape,
        input_output_aliases=aliases,
        compiler_params=_cparams(("arbitrary",)),
        name="decode_attn",
    )(*args)


def _pool_sample_body(sc_ref, u_ref, diff_ref, nc_ref):
    rows = [sc_ref[0, k] for k in range(POOL_STATE)] + [u_ref[j] for j in range(DEC)]
    lane = lax.broadcasted_iota(jnp.int32, rows[0].shape, 1)
    for j in range(DEC):
        t = POOL_STATE + j
        acc = rows[t]
        sums = {}
        for k in range(1, POOL_WINDOWS[-1]):
            acc = acc + rows[t - k]
            if k + 1 in POOL_WINDOWS:
                sums[k + 1] = acc
        mean = sums[POOL_WINDOWS[-1]] / float(POOL_WINDOWS[-1])
        for gi in range(len(POOL_WINDOWS) - 2, -1, -1):
            mean = jnp.where(lane < (gi + 1) * HEAD_DIM, sums[POOL_WINDOWS[gi]] / float(POOL_WINDOWS[gi]), mean)
        diff_ref[j] = mean - rows[t]
    for k in range(POOL_STATE):
        nc_ref[k] = rows[k + DEC]


def _pool_sample(layer, sc_t, u_slabs):
    nseq = u_slabs.shape[1]
    return pl.pallas_call(
        _pool_sample_body,
        grid=(1,),
        in_specs=[pl.BlockSpec((1, POOL_STATE, nseq, WIDTH_C), lambda i: (layer, 0, 0, 0)),
                  pl.BlockSpec((DEC, nseq, WIDTH_C), lambda i: (0, 0, 0))],
        out_specs=(pl.BlockSpec((DEC, nseq, WIDTH_C), lambda i: (0, 0, 0)),
                   pl.BlockSpec((POOL_STATE, nseq, WIDTH_C), lambda i: (0, 0, 0))),
        out_shape=(jax.ShapeDtypeStruct((DEC, nseq, WIDTH_C), F32), jax.ShapeDtypeStruct((POOL_STATE, nseq, WIDTH_C), F32)),
        compiler_params=_cparams(("arbitrary",)),
        name="pool_sample",
    )(sc_t, u_slabs)


def _head_pairs(x, axis):
    shp = x.shape
    x = x.reshape(shp[:axis] + (N_KV_A, REP_A, HEAD_DIM) + shp[axis + 1:])
    x = jnp.swapaxes(x, axis, axis + 1)
    return x.reshape(shp)


def _prep_weights(w_in, w_proj_a, w_pool):
    scale = HEAD_DIM ** -0.5
    pieces = [
        (_head_pairs(w_in[:, :, COL_QA:COL_KA], 2) * scale).astype(BF16),
        w_in[:, :, COL_KA:COL_GA].astype(BF16),
        _head_pairs(w_in[:, :, COL_GA:COL_QB], 2).astype(BF16),
        (w_in[:, :, COL_QB:COL_KB] * scale).astype(BF16),
        w_in[:, :, COL_KB:].astype(BF16),
    ]
    w_bf = jnp.concatenate(pieces, axis=-1)
    wpa_p = _head_pairs(w_proj_a, 1).astype(BF16)
    depth = w_pool.shape[0]
    ng = len(POOL_WINDOWS)
    eye = jnp.eye(ng, dtype=w_pool.dtype)
    wpool_bd = (w_pool[:, :, :, None, :] * eye[None, :, None, :, None]).reshape(depth, WIDTH_C, WIDTH_C).astype(BF16)
    return w_bf, wpa_p, wpool_bd


def kernel(x_prompt, x_sample, state_a, state_b1, state_b2, state_b3, state_c, norm_w, final_norm_w, w_in, sink_a,
           w_proj_a, w_proj_b, w_proj_c, w_pool, pool_scale, w_out):
    n_batch, seq, _ = x_prompt.shape
    nseq, dec, _ = x_sample.shape
    depth = w_in.shape[0]
    assert dec == DEC

    w_bf, wpa_p, wpool_bd = _prep_weights(w_in, w_proj_a, w_pool)
    wpb, wpc, wout = w_proj_b.astype(BF16), w_proj_c.astype(BF16), w_out.astype(BF16)
    nw3 = norm_w.reshape(depth, 1, D_MODEL)
    ps3 = pool_scale.reshape(depth, 1, WIDTH_C)
    fnw = final_norm_w.reshape(1, D_MODEL)
    tbl_a = jnp.asarray(_tables_a())
    tbl_b = jnp.asarray(_tables_b())
    sl_a, sl_b = (jnp.asarray(t) for t in _decode_slopes())
    sink_prompt = sink_a.reshape(depth, N_KV_A, REP_A).transpose(0, 2, 1).reshape(depth, N_HEADS_A)
    sink_dec = jnp.repeat(sink_prompt, DEC, axis=1)[..., None]

    to_t = lambda s: jnp.transpose(s, (0, 1, 2, 4, 5, 3))
    from_t = lambda s: jnp.transpose(s, (0, 1, 2, 5, 3, 4))
    states = tuple(to_t(s) for s in (state_a, state_b1, state_b2, state_b3))
    sc_t = jnp.transpose(state_c, (0, 2, 1, 3))

    hp = x_prompt.reshape(n_batch * seq, D_MODEL)
    hs = jnp.concatenate([x_sample, x_sample], axis=1).reshape(nseq * DEC_ROWS, D_MODEL)
    new_p = [[] for _ in range(5)]
    new_c_s = []
    new_s = None
    for l in range(depth):
        final = l == depth - 1
        out_w = (wpa_p, wpb, wpc, wpool_bd, ps3, wout, fnw)
        (qkva, qkvb1, qkvb2, qkvb3, gates, uc, s_a, s_b1, s_b2, s_b3, s_c) = _inproj_prompt(l, hp, nw3, w_bf, n_batch, seq)
        oa, diff = _attn_a(sink_prompt[l], qkva.reshape(n_batch, seq, SEG_W), uc.reshape(n_batch, seq, WIDTH_C),
                           tbl_a, n_batch, seq)
        ob = _attn_b(*(q.reshape(n_batch, seq, SEG_W) for q in (qkvb1, qkvb2, qkvb3)), tbl_b, n_batch, seq)
        hp = _out_stage(l, hp, oa.reshape(-1, WIDTH_A), ob.reshape(-1, WIDTH_B), diff.reshape(-1, WIDTH_C), gates,
                        *out_w, final)
        for k, s in enumerate((s_a, s_b1, s_b2, s_b3, s_c)):
            new_p[k].append(s)
        qkv_s, gates_s, uc_s, kvt = _inproj_sample(l, hs, nw3, w_bf)
        outs = _decode(l, qkv_s.reshape(nseq, DEC_ROWS, QKV_W), kvt, sink_dec, sl_a, sl_b, states, new_s)
        oa_s, ob_s, new_s = outs[0], outs[1], outs[2:]
        u_slabs = uc_s.reshape(nseq, DEC_ROWS, WIDTH_C)[:, :DEC].transpose(1, 0, 2)
        diff_t, nc_t = _pool_sample(l, sc_t, u_slabs)
        diff_s = jnp.tile(diff_t.transpose(1, 0, 2), (1, 2, 1)).reshape(nseq * DEC_ROWS, WIDTH_C)
        hs = _out_stage(l, hs, oa_s.reshape(-1, WIDTH_A), ob_s.reshape(-1, WIDTH_B), diff_s, gates_s, *out_w, final)
        new_c_s.append(nc_t)

    y_prompt = hp.reshape(n_batch, seq, D_MODEL)
    y_sample = hs.reshape(nseq, DEC_ROWS, D_MODEL)[:, :DEC]
    stack = lambda xs: jnp.stack(xs, axis=0)

    def prompt_state(k, heads, width):
        return from_t(stack(new_p[k]).reshape(depth, n_batch, 2, heads, HEAD_DIM, width))

    return (y_prompt, y_sample,
            prompt_state(0, N_KV_A, WINDOW_A), from_t(new_s[0]),
            prompt_state(1, HEADS_B, B_PAIRS[0][0]), from_t(new_s[1]),
            prompt_state(2, HEADS_B, B_PAIRS[1][0]), from_t(new_s[2]),
            prompt_state(3, HEADS_B, B_PAIRS[2][0]), from_t(new_s[3]),
            stack(new_p[4]), jnp.transpose(stack(new_c_s), (0, 2, 1, 3)))
```

```python
import functools

import numpy as np
import jax
import jax.numpy as jnp
from jax import lax
from jax.experimental import pallas as pl
from jax.experimental.pallas import tpu as pltpu

F32 = jnp.float32
BF16 = jnp.bfloat16

D_MODEL = 1024
HEAD_DIM = 64
N_HEADS_A = 8
N_KV_A = 2
REP_A = N_HEADS_A // N_KV_A
WINDOW_A = 128
B_PAIRS = ((128, 1), (512, 4), (2048, 16))
N_GROUPS_B = 3
HEADS_B = 4
POOL_WINDOWS = (2, 4, 8, 16)
POOL_STATE = 15
WIDTH_A = N_HEADS_A * HEAD_DIM
WIDTH_KV_A = N_KV_A * HEAD_DIM
WIDTH_B = HEADS_B * HEAD_DIM
WIDTH_C = 256
BLOCK = 128
RMS_EPS = 1e-6
NEG_INF = -1e30
LANES = 128
SUBLANES = 8
VMEM_LIMIT = 56 * 1024 * 1024
DEC = 4
DEC_ROWS = 2 * DEC

COL_QA, COL_KA, COL_VA, COL_GA = 0, 512, 640, 768
COL_QB, COL_KB, COL_VB, COL_GB = 1280, 2048, 2816, 3584
COL_UC, COL_GC, COL_MG = 3840, 4096, 4352
D_IN = 7424
SEG_W = 768
QKV_W = 4 * SEG_W
GATE_W = 4096
KVT_A = 0
KVT_B = (256, 768, 1280)
KVT_ROWS = 1792


def _alibi_slopes(n):
    return 2.0 ** (-8.0 * (np.arange(n, dtype=np.float32) + 1.0) / n)


SLOPES_A = _alibi_slopes(N_HEADS_A)
SLOPES_B = _alibi_slopes(N_GROUPS_B * HEADS_B)


WIN_PREV_CUR = 0
WIN_CUR_NEXT = 1
WIN_MASKED_CUR = 2
N_WIN_KINDS = 3


def _band_table(slope, dist_scale, kind):
    qi = np.arange(BLOCK)[:, None]
    ki = np.arange(2 * BLOCK)[None, :]
    own = (ki < BLOCK) if kind == WIN_CUR_NEXT else (ki >= BLOCK)
    dist = (qi - ki) if kind == WIN_CUR_NEXT else (BLOCK + qi - ki)
    valid = (dist >= 0) & (dist <= BLOCK)
    if kind != WIN_PREV_CUR:
        valid &= own
    bias = -(np.float32(slope) * np.float32(dist_scale)) * dist.astype(np.float32)
    return np.where(valid, bias, np.float32(NEG_INF)).astype(np.float32)


def _tables_a():
    return np.stack([np.stack([_band_table(SLOPES_A[g * REP_A + p], 1, kind) for p in range(REP_A) for g in range(N_KV_A)])
                     for kind in (WIN_PREV_CUR, WIN_CUR_NEXT)])


def _tables_b():
    return np.stack([np.stack([np.stack([_band_table(SLOPES_B[g * HEADS_B + h], B_PAIRS[g][1], kind)
                                         for h in range(HEADS_B)]) for g in range(N_GROUPS_B)])
                     for kind in range(N_WIN_KINDS)])


def _decode_slopes():
    rows_a = np.array([SLOPES_A[(r % DEC_ROWS) // DEC * REP_A + r // DEC_ROWS] for r in range(REP_A * DEC_ROWS)], np.float32)
    rows_b = np.array([[SLOPES_B[g * HEADS_B + r // DEC] for r in range(HEADS_B * DEC)] for g in range(N_GROUPS_B)], np.float32)
    return (np.broadcast_to(rows_a[:, None], (REP_A * DEC_ROWS, LANES)).copy(),
            np.broadcast_to(rows_b[:, :, None], (N_GROUPS_B, HEADS_B * DEC, LANES)).copy())


def _cparams(sem):
    return pltpu.CompilerParams(dimension_semantics=sem, vmem_limit_bytes=VMEM_LIMIT)


def _const_spec(shape):
    nd = len(shape)
    return pl.BlockSpec(shape, lambda *_: (0,) * nd, pipeline_mode=pl.Buffered(1))


def _layer_spec(layer, shape):
    nd = len(shape)
    return pl.BlockSpec((1,) + shape, lambda *_: (layer,) + (0,) * nd, pipeline_mode=pl.Buffered(1))


def _rms_h(x, nw_ref):
    ms = jnp.mean(x * x, axis=-1, keepdims=True)
    return (x * lax.rsqrt(ms + RMS_EPS) * nw_ref[0]).astype(BF16)


def _proj(h, w_ref, lo, width):
    return jnp.dot(h, w_ref[0, :, lo:lo + width], preferred_element_type=F32)


def _store_gates(h, w_ref, gates_ref):
    o = 0
    for lo, width in ((COL_GA, WIDTH_A), (COL_GB, WIDTH_B), (COL_GC, WIDTH_C),
                      (COL_MG, D_MODEL), (COL_MG + D_MODEL, D_MODEL), (COL_MG + 2 * D_MODEL, D_MODEL)):
        gates_ref[:, o:o + width] = _proj(h, w_ref, lo, width).astype(BF16)
        o += width


def _inproj_prompt_tile(x, nw_ref, w_ref, qkva_ref, qkvb1_ref, qkvb2_ref, qkvb3_ref, gates_ref, uc_ref,
                        sa_ref, sb1_ref, sb2_ref, sb3_ref, sc_ref, zscr, tm):
    h = _rms_h(x, nw_ref)

    za = _proj(h, w_ref, COL_QA, SEG_W)
    qkva_ref[...] = za.astype(BF16)
    sa_ref[0, 0, 0] = za[tm - WINDOW_A:, WIDTH_A:WIDTH_A + WIDTH_KV_A].T
    sa_ref[0, 0, 1] = za[tm - WINDOW_A:, WIDTH_A + WIDTH_KV_A:].T

    for g, (s_ref, q_ref) in enumerate(((sb1_ref, qkvb1_ref), (sb2_ref, qkvb2_ref), (sb3_ref, qkvb3_ref))):
        width, dil = B_PAIRS[g]
        keep = min(width, tm)
        parts = [_proj(h, w_ref, col + g * WIDTH_B, WIDTH_B) for col in (COL_QB, COL_KB, COL_VB)]
        s_ref[0, 0, 0] = parts[1][tm - keep:, :].T
        s_ref[0, 0, 1] = parts[2][tm - keep:, :].T
        if dil == 1:
            for k, z in enumerate(parts):
                q_ref[:, k * WIDTH_B:(k + 1) * WIDTH_B] = z.astype(BF16)
            continue
        ncol = WIDTH_B // LANES
        for k, z in enumerate(parts):
            for c in range(ncol):
                zscr[k * ncol + c] = z[:, c * LANES:(c + 1) * LANES]
        for r in range(dil):
            for c in range(SEG_W // LANES):
                q_ref[0, r, :, c * LANES:(c + 1) * LANES] = zscr[c, pl.ds(r, tm // dil, stride=dil), :].astype(BF16)

    _store_gates(h, w_ref, gates_ref)
    zu = _proj(h, w_ref, COL_UC, WIDTH_C)
    uc_ref[...] = zu
    sc_ref[0] = zu[tm - POOL_STATE:, :]


N_OUT_IN = 12
N_INPROJ_OUT = 11


def _token_stage_body(*refs, has_out, has_in, final, n_prev, tm):
    refs = list(refs)
    out_in = refs[:N_OUT_IN] if has_out else refs[:1]
    pos = len(out_in)
    if has_in:
        nw_ref, w_ref = refs[pos:pos + 2]
        pos += 2
    pos += n_prev
    if has_out:
        y_ref = refs[pos]
        pos += 1
        x = _out_tile(*out_in, final=final)
        y_ref[...] = x
    else:
        x = out_in[0][...]
    if has_in:
        _inproj_prompt_tile(x, nw_ref, w_ref, *refs[pos:pos + N_INPROJ_OUT + 1], tm)


def _token_stage(n_batch, seq, x2d, out_args=None, in_args=None, prev_states=None, final=False, tm=256):
    nt = seq // tm
    m = n_batch * seq
    row = lambda n, i: (n * nt + i, 0)
    tile = lambda w: pl.BlockSpec((tm, w), row)
    args, in_specs, out_shape, out_specs = [x2d], [tile(D_MODEL)], [], []
    if out_args is not None:
        lo, oa, ob, diff, gates, wpa, wpb, wpc, wpool, ps, wout, fnw = out_args
        args += [oa, ob, diff, gates, wpa, wpb, wpc, wpool, ps, wout, fnw]
        in_specs += [tile(WIDTH_A), tile(WIDTH_B), tile(WIDTH_C), tile(GATE_W),
                     _layer_spec(lo, (WIDTH_A, D_MODEL)), _layer_spec(lo, (WIDTH_B, D_MODEL)),
                     _layer_spec(lo, (WIDTH_C, D_MODEL)), _layer_spec(lo, (WIDTH_C, WIDTH_C)),
                     _layer_spec(lo, (1, WIDTH_C)), _layer_spec(lo, (D_MODEL, D_MODEL)), _const_spec((1, D_MODEL))]
        out_shape.append(jax.ShapeDtypeStruct((m, D_MODEL), F32))
        out_specs.append(tile(D_MODEL))
    aliases = {}
    scratch = []
    if in_args is not None:
        li, nw, w, depth = in_args
        args += [nw, w]
        in_specs += [_layer_spec(li, (1, D_MODEL)), _layer_spec(li, (D_MODEL, D_IN))]
        b2_tiles = B_PAIRS[1][0] // tm
        state = lambda c, wd: jax.ShapeDtypeStruct((depth, n_batch, 2, c, wd), F32)
        first_state = len(out_shape) + 6
        out_shape += [
            jax.ShapeDtypeStruct((m, SEG_W), BF16),
            jax.ShapeDtypeStruct((m, SEG_W), BF16),
            jax.ShapeDtypeStruct((n_batch, 4, seq // 4, SEG_W), BF16),
            jax.ShapeDtypeStruct((n_batch, 16, seq // 16, SEG_W), BF16),
            jax.ShapeDtypeStruct((m, GATE_W), BF16),
            jax.ShapeDtypeStruct((m, WIDTH_C), F32),
            state(WIDTH_KV_A, WINDOW_A), state(WIDTH_B, B_PAIRS[0][0]),
            state(WIDTH_B, B_PAIRS[1][0]), state(WIDTH_B, B_PAIRS[2][0]),
            jax.ShapeDtypeStruct((n_batch, POOL_STATE, WIDTH_C), F32),
        ]
        out_specs += [
            tile(SEG_W), tile(SEG_W),
            pl.BlockSpec((1, 4, tm // 4, SEG_W), lambda n, i: (n, 0, i, 0)),
            pl.BlockSpec((1, 16, tm // 16, SEG_W), lambda n, i: (n, 0, i, 0)),
            tile(GATE_W), tile(WIDTH_C),
            pl.BlockSpec((1, 1, 2, WIDTH_KV_A, WINDOW_A), lambda n, i: (li, n, 0, 0, 0)),
            pl.BlockSpec((1, 1, 2, WIDTH_B, B_PAIRS[0][0]), lambda n, i: (li, n, 0, 0, 0)),
            pl.BlockSpec((1, 1, 2, WIDTH_B, tm), lambda n, i: (li, n, 0, 0, jnp.maximum(i - (nt - b2_tiles), 0))),
            pl.BlockSpec((1, 1, 2, WIDTH_B, tm), lambda n, i: (li, n, 0, 0, i)),
            pl.BlockSpec((1, POOL_STATE, WIDTH_C), lambda n, i: (n, 0, 0)),
        ]
        scratch = [pltpu.VMEM((SEG_W // LANES, tm, LANES), F32)]
        if prev_states is not None:
            aliases = {len(args) + k: first_state + k for k in range(len(prev_states))}
            in_specs += [pl.BlockSpec(memory_space=pl.ANY)] * len(prev_states)
            args += list(prev_states)
    return pl.pallas_call(
        functools.partial(_token_stage_body, has_out=out_args is not None, has_in=in_args is not None, final=final,
                          n_prev=len(aliases), tm=tm),
        grid=(n_batch, nt),
        in_specs=in_specs,
        out_specs=tuple(out_specs),
        out_shape=tuple(out_shape),
        scratch_shapes=scratch,
        input_output_aliases=aliases,
        compiler_params=_cparams(("arbitrary", "arbitrary")),
        name="token_stage",
    )(*args)


def _inproj_sample_body(x_ref, nw_ref, w_ref, qkv_ref, gates_ref, uc_ref, kvt_ref):
    h = _rms_h(x_ref[...], nw_ref)
    za = _proj(h, w_ref, COL_QA, SEG_W)
    qkv_ref[:, 0:SEG_W] = za
    kvt_ref[KVT_A:KVT_A + WIDTH_KV_A, :] = za[:, WIDTH_A:WIDTH_A + WIDTH_KV_A].T
    kvt_ref[KVT_A + WIDTH_KV_A:KVT_A + 2 * WIDTH_KV_A, :] = za[:, WIDTH_A + WIDTH_KV_A:].T
    for g in range(N_GROUPS_B):
        for k, col in enumerate((COL_QB, COL_KB, COL_VB)):
            o = (1 + g) * SEG_W + k * WIDTH_B
            z = _proj(h, w_ref, col + g * WIDTH_B, WIDTH_B)
            qkv_ref[:, o:o + WIDTH_B] = z
            if k > 0:
                r0 = KVT_B[g] + (k - 1) * WIDTH_B
                kvt_ref[r0:r0 + WIDTH_B, :] = z.T
    _store_gates(h, w_ref, gates_ref)
    uc_ref[...] = _proj(h, w_ref, COL_UC, WIDTH_C)


def _inproj_sample(layer, x2d, nw, w, tm=256):
    m = x2d.shape[0]
    row = lambda i: (i, 0)
    return pl.pallas_call(
        _inproj_sample_body,
        grid=(m // tm,),
        in_specs=[pl.BlockSpec((tm, D_MODEL), row), _layer_spec(layer, (1, D_MODEL)), _layer_spec(layer, (D_MODEL, D_IN))],
        out_specs=(pl.BlockSpec((tm, QKV_W), row), pl.BlockSpec((tm, GATE_W), row), pl.BlockSpec((tm, WIDTH_C), row),
                   pl.BlockSpec((KVT_ROWS, tm), lambda i: (0, i))),
        out_shape=(jax.ShapeDtypeStruct((m, QKV_W), F32), jax.ShapeDtypeStruct((m, GATE_W), BF16),
                   jax.ShapeDtypeStruct((m, WIDTH_C), F32), jax.ShapeDtypeStruct((KVT_ROWS, m), F32)),
        compiler_params=_cparams(("arbitrary",)),
        name="inproj_sample",
    )(x2d, nw, w)


def _lane_masks():
    lane = lax.broadcasted_iota(jnp.int32, (1, LANES), 1)
    lo = lane < HEAD_DIM
    mlo = jnp.where(lo, 1.0, 0.0).astype(BF16)
    mhi = jnp.where(lo, 0.0, 1.0).astype(BF16)
    return lo, mlo, mhi


def _attend(q, kwin, vwin, table, sink=None):
    s = lax.dot_general(q, kwin, (((1,), (1,)), ((), ())), preferred_element_type=F32) + table
    m = jnp.max(s, axis=-1, keepdims=True)
    if sink is not None:
        m = jnp.maximum(m, sink)
    e = jnp.exp(s - m)
    l = jnp.sum(e, axis=-1, keepdims=True)
    if sink is not None:
        l = l + jnp.exp(sink - m)
    acc = jnp.dot(e.astype(BF16), vwin, preferred_element_type=F32)
    return acc * (1.0 / l), m + jnp.log(l)


def _pool_diff(e_ref, rows, t0, pos0):
    cur = e_ref[pl.ds(t0, rows), :]
    acc = cur
    sums = {}
    for k in range(1, POOL_WINDOWS[-1]):
        acc = acc + e_ref[pl.ds(t0 - k, rows), :]
        if k + 1 in POOL_WINDOWS:
            sums[k + 1] = acc
    lane = lax.broadcasted_iota(jnp.int32, (rows, WIDTH_C), 1)
    pos = lax.broadcasted_iota(jnp.int32, (rows, WIDTH_C), 0) + pos0
    wsum = sums[POOL_WINDOWS[-1]]
    win = jnp.full((rows, WIDTH_C), POOL_WINDOWS[-1], jnp.int32)
    for gi in range(len(POOL_WINDOWS) - 2, -1, -1):
        sel = lane < (gi + 1) * HEAD_DIM
        wsum = jnp.where(sel, sums[POOL_WINDOWS[gi]], wsum)
        win = jnp.where(sel, POOL_WINDOWS[gi], win)
    cnt = jnp.minimum(pos + 1, win).astype(F32)
    return wsum / cnt - cur


def _attn_a_block(i, sink_ref, qkv_ref, tbl_ref, oa_ref):
    lo, mlo, mhi = _lane_masks()
    kc = WIDTH_A
    vc = WIDTH_A + WIDTH_KV_A
    first = (i == 0).astype(jnp.int32)
    kind = first * WIN_CUR_NEXT
    r0 = pl.multiple_of(i * BLOCK, BLOCK)
    w0 = pl.multiple_of((i - 1 + first) * BLOCK, BLOCK)
    kwin = qkv_ref[0, pl.ds(w0, 2 * BLOCK), kc:kc + WIDTH_KV_A]
    vwin = qkv_ref[0, pl.ds(w0, 2 * BLOCK), vc:vc + WIDTH_KV_A]
    for p in range(REP_A):
        qp = qkv_ref[0, pl.ds(r0, BLOCK), p * LANES:(p + 1) * LANES]
        o_lo, _ = _attend(qp * mlo, kwin, vwin, tbl_ref[kind, 2 * p], sink_ref[2 * p])
        o_hi, _ = _attend(qp * mhi, kwin, vwin, tbl_ref[kind, 2 * p + 1], sink_ref[2 * p + 1])
        oa_ref[0, :, p * LANES:(p + 1) * LANES] = jnp.where(lo, o_lo, o_hi)


def _pool_sequence(uc_ref, diff_ref, e_ref, seq):
    pad = 2 * SUBLANES
    e_ref[0:pad, :] = jnp.zeros((pad, WIDTH_C), F32)
    e_ref[pad:, :] = uc_ref[0]
    rows = 256
    for c in range(seq // rows):
        diff_ref[0, c * rows:(c + 1) * rows, :] = _pool_diff(e_ref, rows, pad + c * rows, c * rows)


def _attn_b_body(q1_ref, q2_ref, q3_ref, tbl_ref, ob_ref, o_nat, l_nat, o_stage, l_stage, *, seq):
    lo, mlo, mhi = _lane_masks()
    npair = WIDTH_B // LANES

    def block(q_ref, r0, w0, table, dst_o, dst_l):
        for p in range(npair):
            qp = q_ref[0, pl.ds(r0, BLOCK), p * LANES:(p + 1) * LANES]
            kwin = q_ref[0, pl.ds(w0, 2 * BLOCK), WIDTH_B + p * LANES:WIDTH_B + (p + 1) * LANES]
            vwin = q_ref[0, pl.ds(w0, 2 * BLOCK), 2 * WIDTH_B + p * LANES:2 * WIDTH_B + (p + 1) * LANES]
            o_lo, l_lo = _attend(qp * mlo, kwin, vwin, table(2 * p))
            o_hi, l_hi = _attend(qp * mhi, kwin, vwin, table(2 * p + 1))
            dst_o[p, pl.ds(r0, BLOCK), :] = jnp.where(lo, o_lo, o_hi)
            dst_l[p, pl.ds(r0, BLOCK), :] = jnp.where(lo, l_lo, l_hi)

    table = lambda kind, g: (lambda u: tbl_ref[kind, g, u])

    block(q1_ref, 0, 0, table(WIN_CUR_NEXT, 0), o_nat.at[0], l_nat.at[0])

    def loop1(i, c):
        r0 = pl.multiple_of(i * BLOCK, BLOCK)
        block(q1_ref, r0, r0 - BLOCK, table(WIN_PREV_CUR, 0), o_nat.at[0], l_nat.at[0])
        return c

    lax.fori_loop(1, seq // BLOCK, loop1, 0, unroll=3)

    sub = seq // B_PAIRS[1][1]
    for r in range(B_PAIRS[1][1]):
        block(q2_ref, r * sub, r * sub, table(WIN_CUR_NEXT, 1), o_stage, l_stage)

    def loop2(i, c):
        for r in range(B_PAIRS[1][1]):
            r0 = pl.multiple_of(r * sub + i * BLOCK, BLOCK)
            block(q2_ref, r0, r0 - BLOCK, table(WIN_PREV_CUR, 1), o_stage, l_stage)
        return c

    lax.fori_loop(1, sub // BLOCK, loop2, 0)
    for r in range(B_PAIRS[1][1]):
        for p in range(npair):
            o_nat[1, p, pl.ds(r, sub, stride=B_PAIRS[1][1]), :] = o_stage[p, r * sub:(r + 1) * sub, :]
            l_nat[1, p, pl.ds(r, sub, stride=B_PAIRS[1][1]), :] = l_stage[p, r * sub:(r + 1) * sub, :]

    dil3 = B_PAIRS[2][1]
    par = 2

    def loop3(rg, c):
        for k in range(par):
            r = rg * par + k
            last = r // (dil3 - 1)
            r0 = pl.multiple_of(r * BLOCK, BLOCK)
            w0 = pl.multiple_of((r - last) * BLOCK, BLOCK)
            block(q3_ref, r0, w0, (lambda u, last=last: tbl_ref[WIN_CUR_NEXT + last, 2, u]), o_stage, l_stage)
        return c

    lax.fori_loop(0, dil3 // par, loop3, 0)
    for r in range(dil3):
        for p in range(npair):
            o_nat[2, p, pl.ds(r, BLOCK, stride=dil3), :] = o_stage[p, r * BLOCK:(r + 1) * BLOCK, :]
            l_nat[2, p, pl.ds(r, BLOCK, stride=dil3), :] = l_stage[p, r * BLOCK:(r + 1) * BLOCK, :]

    rows = 256
    for c in range(seq // rows):
        for p in range(npair):
            ls = [l_nat[g, p, c * rows:(c + 1) * rows, :] for g in range(N_GROUPS_B)]
            mx = jnp.maximum(jnp.maximum(ls[0], ls[1]), ls[2])
            ws = [jnp.exp(x - mx) for x in ls]
            den = ws[0] + ws[1] + ws[2]
            num = sum(ws[g] * o_nat[g, p, c * rows:(c + 1) * rows, :] for g in range(N_GROUPS_B))
            ob_ref[0, c * rows:(c + 1) * rows, p * LANES:(p + 1) * LANES] = num * (1.0 / den)


def _attn_b(q1, q2, q3, tbl, n_batch, seq):
    npair = WIDTH_B // LANES
    return pl.pallas_call(
        functools.partial(_attn_b_body, seq=seq),
        grid=(n_batch,),
        in_specs=[pl.BlockSpec((1, seq, SEG_W), lambda n: (n, 0, 0))] * N_GROUPS_B
                 + [_const_spec((N_WIN_KINDS, N_GROUPS_B, HEADS_B, BLOCK, 2 * BLOCK))],
        out_specs=pl.BlockSpec((1, seq, WIDTH_B), lambda n: (n, 0, 0)),
        out_shape=jax.ShapeDtypeStruct((n_batch, seq, WIDTH_B), F32),
        scratch_shapes=[pltpu.VMEM((N_GROUPS_B, npair, seq, LANES), F32),
                        pltpu.VMEM((N_GROUPS_B, npair, seq, LANES), F32),
                        pltpu.VMEM((npair, seq, LANES), F32),
                        pltpu.VMEM((npair, seq, LANES), F32)],
        compiler_params=_cparams(("arbitrary",)),
        name="attn_b",
    )(q1, q2, q3, tbl)


def _sigmoid(x):
    return 0.5 * jnp.tanh(0.5 * x) + 0.5


def _out_tile(x_ref, oa_ref, ob_ref, diff_ref, g_ref, wpa_ref, wpb_ref, wpc_ref, wpool_ref, ps_ref, wout_ref,
              fnw_ref, *, final):
    def gate(lo, hi):
        return g_ref[:, lo:hi].astype(F32)

    def silu(g):
        return g * _sigmoid(g)

    o = WIDTH_A
    ga, gb, gc = gate(0, o), gate(o, o + WIDTH_B), gate(o + WIDTH_B, o + WIDTH_B + WIDTH_C)
    mbase = o + WIDTH_B + WIDTH_C
    pa = jnp.dot((oa_ref[...] * silu(ga)).astype(BF16), wpa_ref[0], preferred_element_type=F32)
    pb = jnp.dot((ob_ref[...] * silu(gb)).astype(BF16), wpb_ref[0], preferred_element_type=F32)
    oc = jnp.dot(diff_ref[...].astype(BF16), wpool_ref[0], preferred_element_type=F32) * ps_ref[0]
    pc = jnp.dot((oc * silu(gc)).astype(BF16), wpc_ref[0], preferred_element_type=F32)
    m = (_sigmoid(gate(mbase, mbase + D_MODEL)) * pa
         + _sigmoid(gate(mbase + D_MODEL, mbase + 2 * D_MODEL)) * pb
         + _sigmoid(gate(mbase + 2 * D_MODEL, mbase + 3 * D_MODEL)) * pc)
    y = x_ref[...] + jnp.dot(m.astype(BF16), wout_ref[0], preferred_element_type=F32)
    if final:
        ms = jnp.mean(y * y, axis=-1, keepdims=True)
        y = y * lax.rsqrt(ms + RMS_EPS) * fnw_ref[...]
    return y


def _decode_body(*refs, has_prev, seq):
    n_dec = 9
    n_prm = 4
    psink_ref, pqkv_ref, puc_ref, ptbl_ref = refs[n_dec:n_dec + n_prm]
    outs = refs[n_dec + n_prm + (4 if has_prev else 0):]
    poa_ref, pdiff_ref, e_ref = outs[6:]
    i = pl.program_id(1)
    _decode_sequence(pl.program_id(0) * (seq // BLOCK) + i, refs[:n_dec], outs[:6])
    _attn_a_block(i, psink_ref, pqkv_ref, ptbl_ref, poa_ref)

    @pl.when(i == 0)
    def _():
        _pool_sequence(puc_ref, pdiff_ref, e_ref, seq)


def _decode_sequence(n, in_refs, out_refs):
    qkv_ref, kvt_ref, sink_ref, sla_ref, slb_ref, sa_ref, sb1_ref, sb2_ref, sb3_ref = in_refs
    oa_ref, ob_ref, na_ref, nb1_ref, nb2_ref, nb3_ref = out_refs
    new_shift = (LANES - DEC) - (n % (LANES // DEC_ROWS)) * DEC_ROWS

    def shifted_state(s_ref, n_ref, kv, rows, width, kvt_row):
        x = s_ref[0, 0, kv].reshape(rows, width)
        rolled = pltpu.roll(x, width - DEC, axis=1)
        new_t = pltpu.roll(kvt_ref[kvt_row:kvt_row + rows, :], new_shift, axis=1)
        lane = lax.broadcasted_iota(jnp.int32, (rows, LANES), 1)
        last = jnp.where(lane < LANES - DEC, rolled[:, width - LANES:], new_t)
        heads = rows // HEAD_DIM
        if width > LANES:
            n_ref[0, 0, kv, :, :, 0:width - LANES] = rolled[:, 0:width - LANES].reshape(heads, HEAD_DIM, width - LANES)
        n_ref[0, 0, kv, :, :, width - LANES:width] = last.reshape(heads, HEAD_DIM, LANES)
        return x

    def attend(q, kt, vt, knew, vnew, live, slopes, width, dil, sink):
        rows = q.shape[0]
        r_i = lax.broadcasted_iota(jnp.int32, (rows, width), 0)
        t_i = lax.broadcasted_iota(jnp.int32, (rows, width), 1)
        jq = r_i & (DEC - 1)
        s = jnp.dot(q.astype(BF16), kt.astype(BF16), preferred_element_type=F32)
        dist = (float(width) + jq.astype(F32)) - t_i.astype(F32)
        valid = (t_i >= jq) if dil == 1 else ((t_i & (dil - 1)) == jq)
        s = jnp.where(valid, s - slopes * dist, NEG_INF)
        jq1 = jq[:, :1].astype(F32)
        s_new = []
        if dil == 1:
            for j2 in range(DEC):
                kb = jnp.broadcast_to(knew[j2:j2 + 1, :], q.shape)
                d2 = jq1 - float(j2)
                s_new.append(jnp.where(d2 >= 0.0, jnp.sum(q * kb, axis=-1, keepdims=True) - slopes * d2, NEG_INF))
        else:
            s_new.append(jnp.sum(q * knew, axis=-1, keepdims=True))
        m = jnp.max(s, axis=-1, keepdims=True)
        for sn in s_new:
            m = jnp.maximum(m, sn)
        if sink is not None:
            m = jnp.maximum(m, sink)
        e = jnp.exp(s - m)
        l = jnp.sum(e, axis=-1, keepdims=True)
        acc = lax.dot_general(e.astype(BF16), vt.astype(BF16), (((1,), (1,)), ((), ())), preferred_element_type=F32)
        for j2, sn in enumerate(s_new):
            en = jnp.exp(sn - m)
            l = l + en
            vb = jnp.broadcast_to(vnew[j2:j2 + 1, :], q.shape) if dil == 1 else vnew
            acc = acc + en * vb
        if sink is not None:
            l = l + jnp.exp(sink - m)
        return jnp.where(live, acc * (1.0 / l), 0.0), m + jnp.log(l)

    kt = shifted_state(sa_ref, na_ref, 0, WIDTH_KV_A, WINDOW_A, KVT_A)
    vt = shifted_state(sa_ref, na_ref, 1, WIDTH_KV_A, WINDOW_A, KVT_A + WIDTH_KV_A)
    ra = REP_A * DEC_ROWS
    row = lax.broadcasted_iota(jnp.int32, (ra, LANES), 0)
    lane = lax.broadcasted_iota(jnp.int32, (ra, LANES), 1)
    live_a = ((row & (DEC_ROWS - 1)) < DEC) == (lane < HEAD_DIM)
    q32 = jnp.concatenate([qkv_ref[0, :, p * LANES:(p + 1) * LANES] for p in range(REP_A)], axis=0)
    q32 = jnp.where(live_a, q32, 0.0)
    knew = jnp.concatenate([qkv_ref[0, :, WIDTH_A:WIDTH_A + WIDTH_KV_A]] * REP_A, axis=0)
    vnew = jnp.concatenate([qkv_ref[0, :, WIDTH_A + WIDTH_KV_A:SEG_W]] * REP_A, axis=0)
    o32, _ = attend(q32, kt, vt, knew, vnew, live_a, sla_ref[:, 0:1], WINDOW_A, 1, sink_ref[0])
    for p in range(REP_A):
        r8 = o32[p * DEC_ROWS:(p + 1) * DEC_ROWS]
        oa_ref[0, :, p * LANES:(p + 1) * LANES] = r8 + pltpu.roll(r8, DEC, axis=0)

    rb = HEADS_B * DEC
    row = lax.broadcasted_iota(jnp.int32, (rb, WIDTH_B), 0)
    lane = lax.broadcasted_iota(jnp.int32, (rb, WIDTH_B), 1)
    live_b = (lane // HEAD_DIM) == (row // DEC)
    o_g, lse_g = [], []
    for g, (s_ref, n_ref) in enumerate(((sb1_ref, nb1_ref), (sb2_ref, nb2_ref), (sb3_ref, nb3_ref))):
        width, dil = B_PAIRS[g]
        base = (1 + g) * SEG_W
        kt = shifted_state(s_ref, n_ref, 0, WIDTH_B, width, KVT_B[g])
        vt = shifted_state(s_ref, n_ref, 1, WIDTH_B, width, KVT_B[g] + WIDTH_B)
        q16 = jnp.where(live_b, jnp.concatenate([qkv_ref[0, :, base:base + WIDTH_B]] * 2, axis=0), 0.0)
        knew = jnp.concatenate([qkv_ref[0, :, base + WIDTH_B:base + 2 * WIDTH_B]] * 2, axis=0)
        vnew = jnp.concatenate([qkv_ref[0, :, base + 2 * WIDTH_B:base + 3 * WIDTH_B]] * 2, axis=0)
        o16, lse = attend(q16, kt, vt, knew, vnew, live_b, slb_ref[g, :, 0:1], width, dil, None)
        o_g.append(o16)
        lse_g.append(lse)
    mx = jnp.maximum(jnp.maximum(lse_g[0], lse_g[1]), lse_g[2])
    ws = [jnp.exp(x - mx) for x in lse_g]
    o16 = (ws[0] * o_g[0] + ws[1] * o_g[1] + ws[2] * o_g[2]) * (1.0 / (ws[0] + ws[1] + ws[2]))
    x8 = o16[0:DEC_ROWS] + o16[DEC_ROWS:]
    ob_ref[0] = x8 + pltpu.roll(x8, DEC, axis=0)


def _decode(layer, qkv8, kvt, sink32, slopes_a, slopes_b, states, prev, p_sink, p_qkva, p_uc, p_tbl):
    nseq = qkv8.shape[0]
    n_batch, seq = p_qkva.shape[:2]
    nblk = seq // BLOCK
    assert nseq == n_batch * nblk
    dseq = lambda b, i: b * nblk + i
    st = lambda a: pl.BlockSpec((1, 1) + a.shape[2:], lambda b, i: (layer, dseq(b, i), 0, 0, 0, 0))
    per_seq = lambda w: pl.BlockSpec((1, DEC_ROWS, w), lambda b, i: (dseq(b, i), 0, 0))
    whole = lambda w: pl.BlockSpec((1, seq, w), lambda b, i: (b, 0, 0))
    in_specs = [per_seq(QKV_W),
                pl.BlockSpec((KVT_ROWS, LANES), lambda b, i: (0, dseq(b, i) // (LANES // DEC_ROWS))),
                _layer_spec(layer, sink32.shape[1:]), _const_spec(slopes_a.shape), _const_spec(slopes_b.shape)]
    in_specs += [st(a) for a in states]
    in_specs += [pl.BlockSpec(memory_space=pltpu.SMEM), whole(SEG_W), whole(WIDTH_C), _const_spec(p_tbl.shape)]
    args = [qkv8, kvt, sink32, slopes_a, slopes_b, *states, p_sink, p_qkva, p_uc, p_tbl]
    aliases = {}
    if prev is not None:
        in_specs += [pl.BlockSpec(memory_space=pl.ANY)] * len(prev)
        aliases = {len(args) + k: 2 + k for k in range(len(prev))}
        args += list(prev)
    out_specs = ((per_seq(WIDTH_A), per_seq(WIDTH_B)) + tuple(st(a) for a in states)
                 + (pl.BlockSpec((1, BLOCK, WIDTH_A), lambda b, i: (b, i, 0)), whole(WIDTH_C)))
    out_shape = ((jax.ShapeDtypeStruct((nseq, DEC_ROWS, WIDTH_A), F32), jax.ShapeDtypeStruct((nseq, DEC_ROWS, WIDTH_B), F32))
                 + tuple(jax.ShapeDtypeStruct(a.shape, F32) for a in states)
                 + (jax.ShapeDtypeStruct((n_batch, seq, WIDTH_A), F32), jax.ShapeDtypeStruct((n_batch, seq, WIDTH_C), F32)))
    return pl.pallas_call(
        functools.partial(_decode_body, has_prev=prev is not None, seq=seq),
        grid=(n_batch, nblk),
        in_specs=in_specs,
        out_specs=out_specs,
        out_shape=out_shape,
        scratch_shapes=[pltpu.VMEM((seq + 2 * SUBLANES, WIDTH_C), F32)],
        input_output_aliases=aliases,
        compiler_params=_cparams(("arbitrary", "arbitrary")),
        name="decode_attn",
    )(*args)


def _pool_sample_body(sc_ref, u_ref, diff_ref, nc_ref):
    rows = [sc_ref[0, k] for k in range(POOL_STATE)] + [u_ref[j] for j in range(DEC)]
    lane = lax.broadcasted_iota(jnp.int32, rows[0].shape, 1)
    for j in range(DEC):
        t = POOL_STATE + j
        acc = rows[t]
        sums = {}
        for k in range(1, POOL_WINDOWS[-1]):
            acc = acc + rows[t - k]
            if k + 1 in POOL_WINDOWS:
                sums[k + 1] = acc
        mean = sums[POOL_WINDOWS[-1]] / float(POOL_WINDOWS[-1])
        for gi in range(len(POOL_WINDOWS) - 2, -1, -1):
            mean = jnp.where(lane < (gi + 1) * HEAD_DIM, sums[POOL_WINDOWS[gi]] / float(POOL_WINDOWS[gi]), mean)
        diff_ref[j] = mean - rows[t]
    for k in range(POOL_STATE):
        nc_ref[k] = rows[k + DEC]


def _pool_sample(layer, sc_t, u_slabs):
    nseq = u_slabs.shape[1]
    return pl.pallas_call(
        _pool_sample_body,
        grid=(1,),
        in_specs=[pl.BlockSpec((1, POOL_STATE, nseq, WIDTH_C), lambda i: (layer, 0, 0, 0)),
                  pl.BlockSpec((DEC, nseq, WIDTH_C), lambda i: (0, 0, 0))],
        out_specs=(pl.BlockSpec((DEC, nseq, WIDTH_C), lambda i: (0, 0, 0)),
                   pl.BlockSpec((POOL_STATE, nseq, WIDTH_C), lambda i: (0, 0, 0))),
        out_shape=(jax.ShapeDtypeStruct((DEC, nseq, WIDTH_C), F32), jax.ShapeDtypeStruct((POOL_STATE, nseq, WIDTH_C), F32)),
        compiler_params=_cparams(("arbitrary",)),
        name="pool_sample",
    )(sc_t, u_slabs)


def _head_pairs(x, axis):
    shp = x.shape
    x = x.reshape(shp[:axis] + (N_KV_A, REP_A, HEAD_DIM) + shp[axis + 1:])
    x = jnp.swapaxes(x, axis, axis + 1)
    return x.reshape(shp)


def _prep_weights(w_in, w_proj_a, w_pool):
    scale = HEAD_DIM ** -0.5
    pieces = [
        (_head_pairs(w_in[:, :, COL_QA:COL_KA], 2) * scale).astype(BF16),
        w_in[:, :, COL_KA:COL_GA].astype(BF16),
        _head_pairs(w_in[:, :, COL_GA:COL_QB], 2).astype(BF16),
        (w_in[:, :, COL_QB:COL_KB] * scale).astype(BF16),
        w_in[:, :, COL_KB:].astype(BF16),
    ]
    w_bf = jnp.concatenate(pieces, axis=-1)
    wpa_p = _head_pairs(w_proj_a, 1).astype(BF16)
    depth = w_pool.shape[0]
    ng = len(POOL_WINDOWS)
    eye = jnp.eye(ng, dtype=w_pool.dtype)
    wpool_bd = (w_pool[:, :, :, None, :] * eye[None, :, None, :, None]).reshape(depth, WIDTH_C, WIDTH_C).astype(BF16)
    return w_bf, wpa_p, wpool_bd


def kernel(x_prompt, x_sample, state_a, state_b1, state_b2, state_b3, state_c, norm_w, final_norm_w, w_in, sink_a,
           w_proj_a, w_proj_b, w_proj_c, w_pool, pool_scale, w_out):
    n_batch, seq, _ = x_prompt.shape
    nseq, dec, _ = x_sample.shape
    depth = w_in.shape[0]
    assert dec == DEC

    w_bf, wpa_p, wpool_bd = _prep_weights(w_in, w_proj_a, w_pool)
    wpb, wpc, wout = w_proj_b.astype(BF16), w_proj_c.astype(BF16), w_out.astype(BF16)
    nw3 = norm_w.reshape(depth, 1, D_MODEL)
    ps3 = pool_scale.reshape(depth, 1, WIDTH_C)
    fnw = final_norm_w.reshape(1, D_MODEL)
    tbl_a = jnp.asarray(_tables_a())
    tbl_b = jnp.asarray(_tables_b())
    sl_a, sl_b = (jnp.asarray(t) for t in _decode_slopes())
    sink_prompt = sink_a.reshape(depth, N_KV_A, REP_A).transpose(0, 2, 1).reshape(depth, N_HEADS_A)
    sink_dec = jnp.repeat(sink_prompt, DEC, axis=1)[..., None]

    to_t = lambda s: jnp.transpose(s, (0, 1, 2, 4, 5, 3))
    from_t = lambda s: jnp.transpose(s, (0, 1, 2, 5, 3, 4))
    states = tuple(to_t(s) for s in (state_a, state_b1, state_b2, state_b3))
    sc_t = jnp.transpose(state_c, (0, 2, 1, 3))

    hp = x_prompt.reshape(n_batch * seq, D_MODEL)
    hs = jnp.concatenate([x_sample, x_sample], axis=1).reshape(nseq * DEC_ROWS, D_MODEL)
    new_c_p, new_c_s = [], []
    new_s = None
    out_w = (wpa_p, wpb, wpc, wpool_bd, ps3, wout, fnw)
    stage = _token_stage(n_batch, seq, hp, in_args=(0, nw3, w_bf, depth))
    for l in range(depth):
        final = l == depth - 1
        qkva, qkvb1, qkvb2, qkvb3, gates, uc = stage[:6]
        new_p = stage[6:10]
        new_c_p.append(stage[10])
        qkv_s, gates_s, uc_s, kvt = _inproj_sample(l, hs, nw3, w_bf)
        outs = _decode(l, qkv_s.reshape(nseq, DEC_ROWS, QKV_W), kvt, sink_dec, sl_a, sl_b, states, new_s,
                       sink_prompt[l], qkva.reshape(n_batch, seq, SEG_W), uc.reshape(n_batch, seq, WIDTH_C), tbl_a)
        oa_s, ob_s, new_s, oa, diff = outs[0], outs[1], outs[2:6], outs[6], outs[7]
        ob = _attn_b(*(q.reshape(n_batch, seq, SEG_W) for q in (qkvb1, qkvb2, qkvb3)), tbl_b, n_batch, seq)
        out_args = (l, oa.reshape(-1, WIDTH_A), ob.reshape(-1, WIDTH_B), diff.reshape(-1, WIDTH_C), gates) + out_w
        if final:
            hp = _token_stage(n_batch, seq, hp, out_args=out_args, final=True)[0]
        else:
            res = _token_stage(n_batch, seq, hp, out_args=out_args, in_args=(l + 1, nw3, w_bf, depth), prev_states=new_p)
            hp, stage = res[0], res[1:]
        u_slabs = uc_s.reshape(nseq, DEC_ROWS, WIDTH_C)[:, :DEC].transpose(1, 0, 2)
        diff_t, nc_t = _pool_sample(l, sc_t, u_slabs)
        diff_s = jnp.tile(diff_t.transpose(1, 0, 2), (1, 2, 1)).reshape(nseq * DEC_ROWS, WIDTH_C)
        out_args_s = (l, oa_s.reshape(-1, WIDTH_A), ob_s.reshape(-1, WIDTH_B), diff_s, gates_s) + out_w
        hs = _token_stage(1, nseq * DEC_ROWS, hs, out_args=out_args_s, final=final)[0]
        new_c_s.append(nc_t)

    y_prompt = hp.reshape(n_batch, seq, D_MODEL)
    y_sample = hs.reshape(nseq, DEC_ROWS, D_MODEL)[:, :DEC]
    stack = lambda xs: jnp.stack(xs, axis=0)

    def prompt_state(k, heads, width):
        return from_t(new_p[k].reshape(depth, n_batch, 2, heads, HEAD_DIM, width))

    return (y_prompt, y_sample,
            prompt_state(0, N_KV_A, WINDOW_A), from_t(new_s[0]),
            prompt_state(1, HEADS_B, B_PAIRS[0][0]), from_t(new_s[1]),
            prompt_state(2, HEADS_B, B_PAIRS[1][0]), from_t(new_s[2]),
            prompt_state(3, HEADS_B, B_PAIRS[2][0]), from_t(new_s[3]),
            stack(new_c_p), jnp.transpose(stack(new_c_s), (0, 2, 1, 3)))
```

```python
import functools

import numpy as np
import jax
import jax.numpy as jnp
from jax import lax
from jax.experimental import pallas as pl
from jax.experimental.pallas import tpu as pltpu

F32 = jnp.float32
BF16 = jnp.bfloat16

D_MODEL = 1024
HEAD_DIM = 64
N_HEADS_A = 8
N_KV_A = 2
REP_A = N_HEADS_A // N_KV_A
WINDOW_A = 128
B_PAIRS = ((128, 1), (512, 4), (2048, 16))
N_GROUPS_B = 3
HEADS_B = 4
POOL_WINDOWS = (2, 4, 8, 16)
POOL_STATE = 15
WIDTH_A = N_HEADS_A * HEAD_DIM
WIDTH_KV_A = N_KV_A * HEAD_DIM
WIDTH_B = HEADS_B * HEAD_DIM
WIDTH_C = 256
BLOCK = 128
RMS_EPS = 1e-6
NEG_INF = -1e30
LANES = 128
SUBLANES = 8
VMEM_LIMIT = 56 * 1024 * 1024
DEC = 4
DEC_ROWS = 2 * DEC

COL_QA, COL_KA, COL_VA, COL_GA = 0, 512, 640, 768
COL_QB, COL_KB, COL_VB, COL_GB = 1280, 2048, 2816, 3584
COL_UC, COL_GC, COL_MG = 3840, 4096, 4352
D_IN = 7424
SEG_W = 768
QKV_W = 4 * SEG_W
GATE_W = 4096
KVT_A = 0
KVT_B = (256, 768, 1280)
KVT_ROWS = 1792


def _alibi_slopes(n):
    return 2.0 ** (-8.0 * (np.arange(n, dtype=np.float32) + 1.0) / n)


SLOPES_A = _alibi_slopes(N_HEADS_A)
SLOPES_B = _alibi_slopes(N_GROUPS_B * HEADS_B)


WIN_PREV_CUR = 0
WIN_CUR_NEXT = 1
WIN_MASKED_CUR = 2
N_WIN_KINDS = 3


def _band_table(slope, dist_scale, kind):
    qi = np.arange(BLOCK)[:, None]
    ki = np.arange(2 * BLOCK)[None, :]
    own = (ki < BLOCK) if kind == WIN_CUR_NEXT else (ki >= BLOCK)
    dist = (qi - ki) if kind == WIN_CUR_NEXT else (BLOCK + qi - ki)
    valid = (dist >= 0) & (dist <= BLOCK)
    if kind != WIN_PREV_CUR:
        valid &= own
    bias = -(np.float32(slope) * np.float32(dist_scale)) * dist.astype(np.float32)
    return np.where(valid, bias, np.float32(NEG_INF)).astype(np.float32)


def _tables_a():
    return np.stack([np.stack([_band_table(SLOPES_A[g * REP_A + p], 1, kind) for p in range(REP_A) for g in range(N_KV_A)])
                     for kind in (WIN_PREV_CUR, WIN_CUR_NEXT)])


def _tables_b():
    return np.stack([np.stack([np.stack([_band_table(SLOPES_B[g * HEADS_B + h], B_PAIRS[g][1], kind)
                                         for h in range(HEADS_B)]) for g in range(N_GROUPS_B)])
                     for kind in range(N_WIN_KINDS)])


def _decode_slopes():
    rows_a = np.array([SLOPES_A[(r % DEC_ROWS) // DEC * REP_A + r // DEC_ROWS] for r in range(REP_A * DEC_ROWS)], np.float32)
    rows_b = np.array([[SLOPES_B[g * HEADS_B + r // DEC] for r in range(HEADS_B * DEC)] for g in range(N_GROUPS_B)], np.float32)
    return (np.broadcast_to(rows_a[:, None], (REP_A * DEC_ROWS, LANES)).copy(),
            np.broadcast_to(rows_b[:, :, None], (N_GROUPS_B, HEADS_B * DEC, LANES)).copy())


def _cparams(sem):
    return pltpu.CompilerParams(dimension_semantics=sem, vmem_limit_bytes=VMEM_LIMIT)


def _const_spec(shape):
    nd = len(shape)
    return pl.BlockSpec(shape, lambda *_: (0,) * nd, pipeline_mode=pl.Buffered(1))


def _layer_spec(layer, shape):
    nd = len(shape)
    return pl.BlockSpec((1,) + shape, lambda *_: (layer,) + (0,) * nd, pipeline_mode=pl.Buffered(1))


def _rms_h(x, nw_ref):
    ms = jnp.mean(x * x, axis=-1, keepdims=True)
    return (x * lax.rsqrt(ms + RMS_EPS) * nw_ref[0]).astype(BF16)


def _proj(h, w_ref, lo, width):
    return jnp.dot(h, w_ref[0, :, lo:lo + width], preferred_element_type=F32)


def _store_gates(h, w_ref, gates_ref):
    o = 0
    for lo, width in ((COL_GA, WIDTH_A), (COL_GB, WIDTH_B), (COL_GC, WIDTH_C),
                      (COL_MG, D_MODEL), (COL_MG + D_MODEL, D_MODEL), (COL_MG + 2 * D_MODEL, D_MODEL)):
        gates_ref[:, o:o + width] = _proj(h, w_ref, lo, width).astype(BF16)
        o += width


def _inproj_prompt_tile(x, nw_ref, w_ref, qkva_ref, qkvb1_ref, qkvb2_ref, qkvb3_ref, gates_ref, uc_ref,
                        sa_ref, sb1_ref, sb2_ref, sb3_ref, sc_ref, zscr, tm):
    h = _rms_h(x, nw_ref)

    za = _proj(h, w_ref, COL_QA, SEG_W)
    qkva_ref[...] = za.astype(BF16)
    sa_ref[0, 0, 0] = za[tm - WINDOW_A:, WIDTH_A:WIDTH_A + WIDTH_KV_A].T
    sa_ref[0, 0, 1] = za[tm - WINDOW_A:, WIDTH_A + WIDTH_KV_A:].T

    for g, (s_ref, q_ref) in enumerate(((sb1_ref, qkvb1_ref), (sb2_ref, qkvb2_ref), (sb3_ref, qkvb3_ref))):
        width, dil = B_PAIRS[g]
        keep = min(width, tm)
        parts = [_proj(h, w_ref, col + g * WIDTH_B, WIDTH_B) for col in (COL_QB, COL_KB, COL_VB)]
        s_ref[0, 0, 0] = parts[1][tm - keep:, :].T
        s_ref[0, 0, 1] = parts[2][tm - keep:, :].T
        if dil == 1:
            for k, z in enumerate(parts):
                q_ref[:, k * WIDTH_B:(k + 1) * WIDTH_B] = z.astype(BF16)
            continue
        ncol = WIDTH_B // LANES
        for k, z in enumerate(parts):
            for c in range(ncol):
                zscr[k * ncol + c] = z[:, c * LANES:(c + 1) * LANES]
        for r in range(dil):
            for c in range(SEG_W // LANES):
                q_ref[0, r, :, c * LANES:(c + 1) * LANES] = zscr[c, pl.ds(r, tm // dil, stride=dil), :].astype(BF16)

    _store_gates(h, w_ref, gates_ref)
    zu = _proj(h, w_ref, COL_UC, WIDTH_C)
    uc_ref[...] = zu
    sc_ref[0] = zu[tm - POOL_STATE:, :]


N_OUT_IN = 12
N_INPROJ_OUT = 11


def _token_stage_body(*refs, has_out, has_in, final, n_prev, tm):
    refs = list(refs)
    out_in = refs[:N_OUT_IN] if has_out else refs[:1]
    pos = len(out_in)
    if has_in:
        nw_ref, w_ref = refs[pos:pos + 2]
        pos += 2
    pos += n_prev
    if has_out:
        y_ref = refs[pos]
        pos += 1
        x = _out_tile(*out_in, final=final)
        y_ref[...] = x
    else:
        x = out_in[0][...]
    if has_in:
        _inproj_prompt_tile(x, nw_ref, w_ref, *refs[pos:pos + N_INPROJ_OUT + 1], tm)


def _token_stage(n_batch, seq, x2d, out_args=None, in_args=None, prev_states=None, final=False, tm=256):
    nt = seq // tm
    m = n_batch * seq
    row = lambda n, i: (n * nt + i, 0)
    tile = lambda w: pl.BlockSpec((tm, w), row)
    args, in_specs, out_shape, out_specs = [x2d], [tile(D_MODEL)], [], []
    if out_args is not None:
        lo, oa, ob, diff, gates, wpa, wpb, wpc, wpool, ps, wout, fnw = out_args
        args += [oa, ob, diff, gates, wpa, wpb, wpc, wpool, ps, wout, fnw]
        in_specs += [tile(WIDTH_A), tile(WIDTH_B), tile(WIDTH_C), tile(GATE_W),
                     _layer_spec(lo, (WIDTH_A, D_MODEL)), _layer_spec(lo, (WIDTH_B, D_MODEL)),
                     _layer_spec(lo, (WIDTH_C, D_MODEL)), _layer_spec(lo, (WIDTH_C, WIDTH_C)),
                     _layer_spec(lo, (1, WIDTH_C)), _layer_spec(lo, (D_MODEL, D_MODEL)), _const_spec((1, D_MODEL))]
        out_shape.append(jax.ShapeDtypeStruct((m, D_MODEL), F32))
        out_specs.append(tile(D_MODEL))
    aliases = {}
    scratch = []
    if in_args is not None:
        li, nw, w, depth = in_args
        args += [nw, w]
        in_specs += [_layer_spec(li, (1, D_MODEL)), _layer_spec(li, (D_MODEL, D_IN))]
        b2_tiles = B_PAIRS[1][0] // tm
        state = lambda c, wd: jax.ShapeDtypeStruct((depth, n_batch, 2, c, wd), F32)
        first_state = len(out_shape) + 6
        out_shape += [
            jax.ShapeDtypeStruct((m, SEG_W), BF16),
            jax.ShapeDtypeStruct((m, SEG_W), BF16),
            jax.ShapeDtypeStruct((n_batch, 4, seq // 4, SEG_W), BF16),
            jax.ShapeDtypeStruct((n_batch, 16, seq // 16, SEG_W), BF16),
            jax.ShapeDtypeStruct((m, GATE_W), BF16),
            jax.ShapeDtypeStruct((m, WIDTH_C), F32),
            state(WIDTH_KV_A, WINDOW_A), state(WIDTH_B, B_PAIRS[0][0]),
            state(WIDTH_B, B_PAIRS[1][0]), state(WIDTH_B, B_PAIRS[2][0]),
            jax.ShapeDtypeStruct((n_batch, POOL_STATE, WIDTH_C), F32),
        ]
        out_specs += [
            tile(SEG_W), tile(SEG_W),
            pl.BlockSpec((1, 4, tm // 4, SEG_W), lambda n, i: (n, 0, i, 0)),
            pl.BlockSpec((1, 16, tm // 16, SEG_W), lambda n, i: (n, 0, i, 0)),
            tile(GATE_W), tile(WIDTH_C),
            pl.BlockSpec((1, 1, 2, WIDTH_KV_A, WINDOW_A), lambda n, i: (li, n, 0, 0, 0)),
            pl.BlockSpec((1, 1, 2, WIDTH_B, B_PAIRS[0][0]), lambda n, i: (li, n, 0, 0, 0)),
            pl.BlockSpec((1, 1, 2, WIDTH_B, tm), lambda n, i: (li, n, 0, 0, jnp.maximum(i - (nt - b2_tiles), 0))),
            pl.BlockSpec((1, 1, 2, WIDTH_B, tm), lambda n, i: (li, n, 0, 0, i)),
            pl.BlockSpec((1, POOL_STATE, WIDTH_C), lambda n, i: (n, 0, 0)),
        ]
        scratch = [pltpu.VMEM((SEG_W // LANES, tm, LANES), F32)]
        if prev_states is not None:
            aliases = {len(args) + k: first_state + k for k in range(len(prev_states))}
            in_specs += [pl.BlockSpec(memory_space=pl.ANY)] * len(prev_states)
            args += list(prev_states)
    return pl.pallas_call(
        functools.partial(_token_stage_body, has_out=out_args is not None, has_in=in_args is not None, final=final,
                          n_prev=len(aliases), tm=tm),
        grid=(n_batch, nt),
        in_specs=in_specs,
        out_specs=tuple(out_specs),
        out_shape=tuple(out_shape),
        scratch_shapes=scratch,
        input_output_aliases=aliases,
        compiler_params=_cparams(("arbitrary", "arbitrary")),
        name="token_stage",
    )(*args)


def _inproj_sample_body(x_ref, nw_ref, w_ref, qkv_ref, gates_ref, uc_ref, kvt_ref):
    h = _rms_h(x_ref[...], nw_ref)
    za = _proj(h, w_ref, COL_QA, SEG_W)
    qkv_ref[:, 0:SEG_W] = za
    kvt_ref[KVT_A:KVT_A + WIDTH_KV_A, :] = za[:, WIDTH_A:WIDTH_A + WIDTH_KV_A].T
    kvt_ref[KVT_A + WIDTH_KV_A:KVT_A + 2 * WIDTH_KV_A, :] = za[:, WIDTH_A + WIDTH_KV_A:].T
    for g in range(N_GROUPS_B):
        for k, col in enumerate((COL_QB, COL_KB, COL_VB)):
            o = (1 + g) * SEG_W + k * WIDTH_B
            z = _proj(h, w_ref, col + g * WIDTH_B, WIDTH_B)
            qkv_ref[:, o:o + WIDTH_B] = z
            if k > 0:
                r0 = KVT_B[g] + (k - 1) * WIDTH_B
                kvt_ref[r0:r0 + WIDTH_B, :] = z.T
    _store_gates(h, w_ref, gates_ref)
    uc_ref[...] = _proj(h, w_ref, COL_UC, WIDTH_C)


def _inproj_sample(layer, x2d, nw, w, tm=256):
    m = x2d.shape[0]
    row = lambda i: (i, 0)
    return pl.pallas_call(
        _inproj_sample_body,
        grid=(m // tm,),
        in_specs=[pl.BlockSpec((tm, D_MODEL), row), _layer_spec(layer, (1, D_MODEL)), _layer_spec(layer, (D_MODEL, D_IN))],
        out_specs=(pl.BlockSpec((tm, QKV_W), row), pl.BlockSpec((tm, GATE_W), row), pl.BlockSpec((tm, WIDTH_C), row),
                   pl.BlockSpec((KVT_ROWS, tm), lambda i: (0, i))),
        out_shape=(jax.ShapeDtypeStruct((m, QKV_W), F32), jax.ShapeDtypeStruct((m, GATE_W), BF16),
                   jax.ShapeDtypeStruct((m, WIDTH_C), F32), jax.ShapeDtypeStruct((KVT_ROWS, m), F32)),
        compiler_params=_cparams(("arbitrary",)),
        name="inproj_sample",
    )(x2d, nw, w)


def _lane_masks():
    lane = lax.broadcasted_iota(jnp.int32, (1, LANES), 1)
    lo = lane < HEAD_DIM
    mlo = jnp.where(lo, 1.0, 0.0).astype(BF16)
    mhi = jnp.where(lo, 0.0, 1.0).astype(BF16)
    return lo, mlo, mhi


def _attend(q, kwin, vwin, table, sink=None):
    s = lax.dot_general(q, kwin, (((1,), (1,)), ((), ())), preferred_element_type=F32) + table
    m = jnp.max(s, axis=-1, keepdims=True)
    if sink is not None:
        m = jnp.maximum(m, sink)
    e = jnp.exp(s - m)
    l = jnp.sum(e, axis=-1, keepdims=True)
    if sink is not None:
        l = l + jnp.exp(sink - m)
    acc = jnp.dot(e.astype(BF16), vwin, preferred_element_type=F32)
    return acc * (1.0 / l), m + jnp.log(l)


def _pool_diff(e_ref, rows, t0, pos0):
    cur = e_ref[pl.ds(t0, rows), :]
    acc = cur
    sums = {}
    for k in range(1, POOL_WINDOWS[-1]):
        acc = acc + e_ref[pl.ds(t0 - k, rows), :]
        if k + 1 in POOL_WINDOWS:
            sums[k + 1] = acc
    lane = lax.broadcasted_iota(jnp.int32, (rows, WIDTH_C), 1)
    pos = lax.broadcasted_iota(jnp.int32, (rows, WIDTH_C), 0) + pos0
    wsum = sums[POOL_WINDOWS[-1]]
    win = jnp.full((rows, WIDTH_C), POOL_WINDOWS[-1], jnp.int32)
    for gi in range(len(POOL_WINDOWS) - 2, -1, -1):
        sel = lane < (gi + 1) * HEAD_DIM
        wsum = jnp.where(sel, sums[POOL_WINDOWS[gi]], wsum)
        win = jnp.where(sel, POOL_WINDOWS[gi], win)
    cnt = jnp.minimum(pos + 1, win).astype(F32)
    return wsum / cnt - cur


def _attn_a_block(i, sink_ref, qkv_ref, tbl_ref, oa_ref):
    lo, mlo, mhi = _lane_masks()
    kc = WIDTH_A
    vc = WIDTH_A + WIDTH_KV_A
    first = (i == 0).astype(jnp.int32)
    kind = first * WIN_CUR_NEXT
    r0 = pl.multiple_of(i * BLOCK, BLOCK)
    w0 = pl.multiple_of((i - 1 + first) * BLOCK, BLOCK)
    kwin = qkv_ref[0, pl.ds(w0, 2 * BLOCK), kc:kc + WIDTH_KV_A]
    vwin = qkv_ref[0, pl.ds(w0, 2 * BLOCK), vc:vc + WIDTH_KV_A]
    for p in range(REP_A):
        qp = qkv_ref[0, pl.ds(r0, BLOCK), p * LANES:(p + 1) * LANES]
        o_lo, _ = _attend(qp * mlo, kwin, vwin, tbl_ref[kind, 2 * p], sink_ref[2 * p])
        o_hi, _ = _attend(qp * mhi, kwin, vwin, tbl_ref[kind, 2 * p + 1], sink_ref[2 * p + 1])
        oa_ref[0, :, p * LANES:(p + 1) * LANES] = jnp.where(lo, o_lo, o_hi)


def _pool_sequence(uc_ref, diff_ref, e_ref, seq):
    pad = 2 * SUBLANES
    e_ref[0:pad, :] = jnp.zeros((pad, WIDTH_C), F32)
    e_ref[pad:, :] = uc_ref[0]
    rows = 256
    for c in range(seq // rows):
        diff_ref[0, c * rows:(c + 1) * rows, :] = _pool_diff(e_ref, rows, pad + c * rows, c * rows)


def _attn_b_body(q1_ref, q2_ref, q3_ref, tbl_ref, ob_ref, o_nat, l_nat, o_stage, l_stage, *, seq):
    lo, mlo, mhi = _lane_masks()
    npair = WIDTH_B // LANES

    def block(q_ref, r0, w0, table, dst_o, dst_l):
        for p in range(npair):
            qp = q_ref[0, pl.ds(r0, BLOCK), p * LANES:(p + 1) * LANES]
            kwin = q_ref[0, pl.ds(w0, 2 * BLOCK), WIDTH_B + p * LANES:WIDTH_B + (p + 1) * LANES]
            vwin = q_ref[0, pl.ds(w0, 2 * BLOCK), 2 * WIDTH_B + p * LANES:2 * WIDTH_B + (p + 1) * LANES]
            o_lo, l_lo = _attend(qp * mlo, kwin, vwin, table(2 * p))
            o_hi, l_hi = _attend(qp * mhi, kwin, vwin, table(2 * p + 1))
            dst_o[p, pl.ds(r0, BLOCK), :] = jnp.where(lo, o_lo, o_hi)
            dst_l[p, pl.ds(r0, BLOCK), :] = jnp.where(lo, l_lo, l_hi)

    table = lambda kind, g: (lambda u: tbl_ref[kind, g, u])

    block(q1_ref, 0, 0, table(WIN_CUR_NEXT, 0), o_nat.at[0], l_nat.at[0])

    def loop1(i, c):
        r0 = pl.multiple_of(i * BLOCK, BLOCK)
        block(q1_ref, r0, r0 - BLOCK, table(WIN_PREV_CUR, 0), o_nat.at[0], l_nat.at[0])
        return c

    lax.fori_loop(1, seq // BLOCK, loop1, 0, unroll=3)

    sub = seq // B_PAIRS[1][1]
    for r in range(B_PAIRS[1][1]):
        block(q2_ref, r * sub, r * sub, table(WIN_CUR_NEXT, 1), o_stage, l_stage)

    def loop2(i, c):
        for r in range(B_PAIRS[1][1]):
            r0 = pl.multiple_of(r * sub + i * BLOCK, BLOCK)
            block(q2_ref, r0, r0 - BLOCK, table(WIN_PREV_CUR, 1), o_stage, l_stage)
        return c

    lax.fori_loop(1, sub // BLOCK, loop2, 0)
    for r in range(B_PAIRS[1][1]):
        for p in range(npair):
            o_nat[1, p, pl.ds(r, sub, stride=B_PAIRS[1][1]), :] = o_stage[p, r * sub:(r + 1) * sub, :]
            l_nat[1, p, pl.ds(r, sub, stride=B_PAIRS[1][1]), :] = l_stage[p, r * sub:(r + 1) * sub, :]

    dil3 = B_PAIRS[2][1]
    par = 2

    def loop3(rg, c):
        for k in range(par):
            r = rg * par + k
            last = r // (dil3 - 1)
            r0 = pl.multiple_of(r * BLOCK, BLOCK)
            w0 = pl.multiple_of((r - last) * BLOCK, BLOCK)
            block(q3_ref, r0, w0, (lambda u, last=last: tbl_ref[WIN_CUR_NEXT + last, 2, u]), o_stage, l_stage)
        return c

    lax.fori_loop(0, dil3 // par, loop3, 0)
    for r in range(dil3):
        for p in range(npair):
            o_nat[2, p, pl.ds(r, BLOCK, stride=dil3), :] = o_stage[p, r * BLOCK:(r + 1) * BLOCK, :]
            l_nat[2, p, pl.ds(r, BLOCK, stride=dil3), :] = l_stage[p, r * BLOCK:(r + 1) * BLOCK, :]

    rows = 256
    for c in range(seq // rows):
        for p in range(npair):
            ls = [l_nat[g, p, c * rows:(c + 1) * rows, :] for g in range(N_GROUPS_B)]
            mx = jnp.maximum(jnp.maximum(ls[0], ls[1]), ls[2])
            ws = [jnp.exp(x - mx) for x in ls]
            den = ws[0] + ws[1] + ws[2]
            num = sum(ws[g] * o_nat[g, p, c * rows:(c + 1) * rows, :] for g in range(N_GROUPS_B))
            ob_ref[0, c * rows:(c + 1) * rows, p * LANES:(p + 1) * LANES] = num * (1.0 / den)


def _attn_b(q1, q2, q3, tbl, n_batch, seq):
    npair = WIDTH_B // LANES
    return pl.pallas_call(
        functools.partial(_attn_b_body, seq=seq),
        grid=(n_batch,),
        in_specs=[pl.BlockSpec((1, seq, SEG_W), lambda n: (n, 0, 0))] * N_GROUPS_B
                 + [_const_spec((N_WIN_KINDS, N_GROUPS_B, HEADS_B, BLOCK, 2 * BLOCK))],
        out_specs=pl.BlockSpec((1, seq, WIDTH_B), lambda n: (n, 0, 0)),
        out_shape=jax.ShapeDtypeStruct((n_batch, seq, WIDTH_B), F32),
        scratch_shapes=[pltpu.VMEM((N_GROUPS_B, npair, seq, LANES), F32),
                        pltpu.VMEM((N_GROUPS_B, npair, seq, LANES), F32),
                        pltpu.VMEM((npair, seq, LANES), F32),
                        pltpu.VMEM((npair, seq, LANES), F32)],
        compiler_params=_cparams(("arbitrary",)),
        name="attn_b",
    )(q1, q2, q3, tbl)


def _sigmoid(x):
    return 0.5 * jnp.tanh(0.5 * x) + 0.5


def _out_tile(x_ref, oa_ref, ob_ref, diff_ref, g_ref, wpa_ref, wpb_ref, wpc_ref, wpool_ref, ps_ref, wout_ref,
              fnw_ref, *, final):
    def gate(lo, hi):
        return g_ref[:, lo:hi].astype(F32)

    def silu(g):
        return g * _sigmoid(g)

    o = WIDTH_A
    ga, gb, gc = gate(0, o), gate(o, o + WIDTH_B), gate(o + WIDTH_B, o + WIDTH_B + WIDTH_C)
    mbase = o + WIDTH_B + WIDTH_C
    pa = jnp.dot((oa_ref[...] * silu(ga)).astype(BF16), wpa_ref[0], preferred_element_type=F32)
    pb = jnp.dot((ob_ref[...] * silu(gb)).astype(BF16), wpb_ref[0], preferred_element_type=F32)
    oc = jnp.dot(diff_ref[...].astype(BF16), wpool_ref[0], preferred_element_type=F32) * ps_ref[0]
    pc = jnp.dot((oc * silu(gc)).astype(BF16), wpc_ref[0], preferred_element_type=F32)
    m = (_sigmoid(gate(mbase, mbase + D_MODEL)) * pa
         + _sigmoid(gate(mbase + D_MODEL, mbase + 2 * D_MODEL)) * pb
         + _sigmoid(gate(mbase + 2 * D_MODEL, mbase + 3 * D_MODEL)) * pc)
    y = x_ref[...] + jnp.dot(m.astype(BF16), wout_ref[0], preferred_element_type=F32)
    if final:
        ms = jnp.mean(y * y, axis=-1, keepdims=True)
        y = y * lax.rsqrt(ms + RMS_EPS) * fnw_ref[...]
    return y


def _decode_body(*refs, has_prev, seq):
    n_dec = 9
    n_prm = 4
    psink_ref, pqkv_ref, puc_ref, ptbl_ref = refs[n_dec:n_dec + n_prm]
    outs = refs[n_dec + n_prm + (4 if has_prev else 0):]
    poa_ref, pdiff_ref, e_ref = outs[6:]
    i = pl.program_id(1)
    _decode_sequence(pl.program_id(0) * (seq // BLOCK) + i, refs[:n_dec], outs[:6])

    @pl.when(i >= 0)
    def _():
        _attn_a_block(i, psink_ref, pqkv_ref, ptbl_ref, poa_ref)

    @pl.when(i == 0)
    def _():
        _pool_sequence(puc_ref, pdiff_ref, e_ref, seq)


def _decode_sequence(n, in_refs, out_refs):
    qkv_ref, kvt_ref, sink_ref, sla_ref, slb_ref, sa_ref, sb1_ref, sb2_ref, sb3_ref = in_refs
    oa_ref, ob_ref, na_ref, nb1_ref, nb2_ref, nb3_ref = out_refs
    new_shift = (LANES - DEC) - (n % (LANES // DEC_ROWS)) * DEC_ROWS

    def shifted_state(s_ref, n_ref, kv, rows, width, kvt_row):
        x = s_ref[0, 0, kv].reshape(rows, width)
        rolled = pltpu.roll(x, width - DEC, axis=1)
        new_t = pltpu.roll(kvt_ref[kvt_row:kvt_row + rows, :], new_shift, axis=1)
        lane = lax.broadcasted_iota(jnp.int32, (rows, LANES), 1)
        last = jnp.where(lane < LANES - DEC, rolled[:, width - LANES:], new_t)
        heads = rows // HEAD_DIM
        if width > LANES:
            n_ref[0, 0, kv, :, :, 0:width - LANES] = rolled[:, 0:width - LANES].reshape(heads, HEAD_DIM, width - LANES)
        n_ref[0, 0, kv, :, :, width - LANES:width] = last.reshape(heads, HEAD_DIM, LANES)
        return x

    def attend(q, kt, vt, knew, vnew, live, slopes, width, dil, sink):
        rows = q.shape[0]
        r_i = lax.broadcasted_iota(jnp.int32, (rows, width), 0)
        t_i = lax.broadcasted_iota(jnp.int32, (rows, width), 1)
        jq = r_i & (DEC - 1)
        s = jnp.dot(q.astype(BF16), kt.astype(BF16), preferred_element_type=F32)
        dist = (float(width) + jq.astype(F32)) - t_i.astype(F32)
        valid = (t_i >= jq) if dil == 1 else ((t_i & (dil - 1)) == jq)
        s = jnp.where(valid, s - slopes * dist, NEG_INF)
        jq1 = jq[:, :1].astype(F32)
        s_new = []
        if dil == 1:
            for j2 in range(DEC):
                kb = jnp.broadcast_to(knew[j2:j2 + 1, :], q.shape)
                d2 = jq1 - float(j2)
                s_new.append(jnp.where(d2 >= 0.0, jnp.sum(q * kb, axis=-1, keepdims=True) - slopes * d2, NEG_INF))
        else:
            s_new.append(jnp.sum(q * knew, axis=-1, keepdims=True))
        m = jnp.max(s, axis=-1, keepdims=True)
        for sn in s_new:
            m = jnp.maximum(m, sn)
        if sink is not None:
            m = jnp.maximum(m, sink)
        e = jnp.exp(s - m)
        l = jnp.sum(e, axis=-1, keepdims=True)
        acc = lax.dot_general(e.astype(BF16), vt.astype(BF16), (((1,), (1,)), ((), ())), preferred_element_type=F32)
        for j2, sn in enumerate(s_new):
            en = jnp.exp(sn - m)
            l = l + en
            vb = jnp.broadcast_to(vnew[j2:j2 + 1, :], q.shape) if dil == 1 else vnew
            acc = acc + en * vb
        if sink is not None:
            l = l + jnp.exp(sink - m)
        return jnp.where(live, acc * (1.0 / l), 0.0), m + jnp.log(l)

    kt = shifted_state(sa_ref, na_ref, 0, WIDTH_KV_A, WINDOW_A, KVT_A)
    vt = shifted_state(sa_ref, na_ref, 1, WIDTH_KV_A, WINDOW_A, KVT_A + WIDTH_KV_A)
    ra = REP_A * DEC_ROWS
    row = lax.broadcasted_iota(jnp.int32, (ra, LANES), 0)
    lane = lax.broadcasted_iota(jnp.int32, (ra, LANES), 1)
    live_a = ((row & (DEC_ROWS - 1)) < DEC) == (lane < HEAD_DIM)
    q32 = jnp.concatenate([qkv_ref[0, :, p * LANES:(p + 1) * LANES] for p in range(REP_A)], axis=0)
    q32 = jnp.where(live_a, q32, 0.0)
    knew = jnp.concatenate([qkv_ref[0, :, WIDTH_A:WIDTH_A + WIDTH_KV_A]] * REP_A, axis=0)
    vnew = jnp.concatenate([qkv_ref[0, :, WIDTH_A + WIDTH_KV_A:SEG_W]] * REP_A, axis=0)
    o32, _ = attend(q32, kt, vt, knew, vnew, live_a, sla_ref[:, 0:1], WINDOW_A, 1, sink_ref[0])
    for p in range(REP_A):
        r8 = o32[p * DEC_ROWS:(p + 1) * DEC_ROWS]
        oa_ref[0, :, p * LANES:(p + 1) * LANES] = r8 + pltpu.roll(r8, DEC, axis=0)

    rb = HEADS_B * DEC
    row = lax.broadcasted_iota(jnp.int32, (rb, WIDTH_B), 0)
    lane = lax.broadcasted_iota(jnp.int32, (rb, WIDTH_B), 1)
    live_b = (lane // HEAD_DIM) == (row // DEC)
    o_g, lse_g = [], []
    for g, (s_ref, n_ref) in enumerate(((sb1_ref, nb1_ref), (sb2_ref, nb2_ref), (sb3_ref, nb3_ref))):
        width, dil = B_PAIRS[g]
        base = (1 + g) * SEG_W
        kt = shifted_state(s_ref, n_ref, 0, WIDTH_B, width, KVT_B[g])
        vt = shifted_state(s_ref, n_ref, 1, WIDTH_B, width, KVT_B[g] + WIDTH_B)
        q16 = jnp.where(live_b, jnp.concatenate([qkv_ref[0, :, base:base + WIDTH_B]] * 2, axis=0), 0.0)
        knew = jnp.concatenate([qkv_ref[0, :, base + WIDTH_B:base + 2 * WIDTH_B]] * 2, axis=0)
        vnew = jnp.concatenate([qkv_ref[0, :, base + 2 * WIDTH_B:base + 3 * WIDTH_B]] * 2, axis=0)
        o16, lse = attend(q16, kt, vt, knew, vnew, live_b, slb_ref[g, :, 0:1], width, dil, None)
        o_g.append(o16)
        lse_g.append(lse)
    mx = jnp.maximum(jnp.maximum(lse_g[0], lse_g[1]), lse_g[2])
    ws = [jnp.exp(x - mx) for x in lse_g]
    o16 = (ws[0] * o_g[0] + ws[1] * o_g[1] + ws[2] * o_g[2]) * (1.0 / (ws[0] + ws[1] + ws[2]))
    x8 = o16[0:DEC_ROWS] + o16[DEC_ROWS:]
    ob_ref[0] = x8 + pltpu.roll(x8, DEC, axis=0)


def _decode(layer, qkv8, kvt, sink32, slopes_a, slopes_b, states, prev, p_sink, p_qkva, p_uc, p_tbl):
    nseq = qkv8.shape[0]
    n_batch, seq = p_qkva.shape[:2]
    nblk = seq // BLOCK
    assert nseq == n_batch * nblk
    dseq = lambda b, i: b * nblk + i
    st = lambda a: pl.BlockSpec((1, 1) + a.shape[2:], lambda b, i: (layer, dseq(b, i), 0, 0, 0, 0))
    per_seq = lambda w: pl.BlockSpec((1, DEC_ROWS, w), lambda b, i: (dseq(b, i), 0, 0))
    whole = lambda w: pl.BlockSpec((1, seq, w), lambda b, i: (b, 0, 0))
    in_specs = [per_seq(QKV_W),
                pl.BlockSpec((KVT_ROWS, LANES), lambda b, i: (0, dseq(b, i) // (LANES // DEC_ROWS))),
                _layer_spec(layer, sink32.shape[1:]), _const_spec(slopes_a.shape), _const_spec(slopes_b.shape)]
    in_specs += [st(a) for a in states]
    in_specs += [pl.BlockSpec(memory_space=pltpu.SMEM), whole(SEG_W), whole(WIDTH_C), _const_spec(p_tbl.shape)]
    args = [qkv8, kvt, sink32, slopes_a, slopes_b, *states, p_sink, p_qkva, p_uc, p_tbl]
    aliases = {}
    if prev is not None:
        in_specs += [pl.BlockSpec(memory_space=pl.ANY)] * len(prev)
        aliases = {len(args) + k: 2 + k for k in range(len(prev))}
        args += list(prev)
    out_specs = ((per_seq(WIDTH_A), per_seq(WIDTH_B)) + tuple(st(a) for a in states)
                 + (pl.BlockSpec((1, BLOCK, WIDTH_A), lambda b, i: (b, i, 0)), whole(WIDTH_C)))
    out_shape = ((jax.ShapeDtypeStruct((nseq, DEC_ROWS, WIDTH_A), F32), jax.ShapeDtypeStruct((nseq, DEC_ROWS, WIDTH_B), F32))
                 + tuple(jax.ShapeDtypeStruct(a.shape, F32) for a in states)
                 + (jax.ShapeDtypeStruct((n_batch, seq, WIDTH_A), F32), jax.ShapeDtypeStruct((n_batch, seq, WIDTH_C), F32)))
    return pl.pallas_call(
        functools.partial(_decode_body, has_prev=prev is not None, seq=seq),
        grid=(n_batch, nblk),
        in_specs=in_specs,
        out_specs=out_specs,
        out_shape=out_shape,
        scratch_shapes=[pltpu.VMEM((seq + 2 * SUBLANES, WIDTH_C), F32)],
        input_output_aliases=aliases,
        compiler_params=_cparams(("arbitrary", "arbitrary")),
        name="decode_attn",
    )(*args)


def _pool_sample_body(sc_ref, u_ref, diff_ref, nc_ref):
    rows = [sc_ref[0, k] for k in range(POOL_STATE)] + [u_ref[j] for j in range(DEC)]
    lane = lax.broadcasted_iota(jnp.int32, rows[0].shape, 1)
    for j in range(DEC):
        t = POOL_STATE + j
        acc = rows[t]
        sums = {}
        for k in range(1, POOL_WINDOWS[-1]):
            acc = acc + rows[t - k]
            if k + 1 in POOL_WINDOWS:
                sums[k + 1] = acc
        mean = sums[POOL_WINDOWS[-1]] / float(POOL_WINDOWS[-1])
        for gi in range(len(POOL_WINDOWS) - 2, -1, -1):
            mean = jnp.where(lane < (gi + 1) * HEAD_DIM, sums[POOL_WINDOWS[gi]] / float(POOL_WINDOWS[gi]), mean)
        diff_ref[j] = mean - rows[t]
    for k in range(POOL_STATE):
        nc_ref[k] = rows[k + DEC]


def _pool_sample(layer, sc_t, u_slabs):
    nseq = u_slabs.shape[1]
    return pl.pallas_call(
        _pool_sample_body,
        grid=(1,),
        in_specs=[pl.BlockSpec((1, POOL_STATE, nseq, WIDTH_C), lambda i: (layer, 0, 0, 0)),
                  pl.BlockSpec((DEC, nseq, WIDTH_C), lambda i: (0, 0, 0))],
        out_specs=(pl.BlockSpec((DEC, nseq, WIDTH_C), lambda i: (0, 0, 0)),
                   pl.BlockSpec((POOL_STATE, nseq, WIDTH_C), lambda i: (0, 0, 0))),
        out_shape=(jax.ShapeDtypeStruct((DEC, nseq, WIDTH_C), F32), jax.ShapeDtypeStruct((POOL_STATE, nseq, WIDTH_C), F32)),
        compiler_params=_cparams(("arbitrary",)),
        name="pool_sample",
    )(sc_t, u_slabs)


def _head_pairs(x, axis):
    shp = x.shape
    x = x.reshape(shp[:axis] + (N_KV_A, REP_A, HEAD_DIM) + shp[axis + 1:])
    x = jnp.swapaxes(x, axis, axis + 1)
    return x.reshape(shp)


def _prep_weights(w_in, w_proj_a, w_pool):
    scale = HEAD_DIM ** -0.5
    pieces = [
        (_head_pairs(w_in[:, :, COL_QA:COL_KA], 2) * scale).astype(BF16),
        w_in[:, :, COL_KA:COL_GA].astype(BF16),
        _head_pairs(w_in[:, :, COL_GA:COL_QB], 2).astype(BF16),
        (w_in[:, :, COL_QB:COL_KB] * scale).astype(BF16),
        w_in[:, :, COL_KB:].astype(BF16),
    ]
    w_bf = jnp.concatenate(pieces, axis=-1)
    wpa_p = _head_pairs(w_proj_a, 1).astype(BF16)
    depth = w_pool.shape[0]
    ng = len(POOL_WINDOWS)
    eye = jnp.eye(ng, dtype=w_pool.dtype)
    wpool_bd = (w_pool[:, :, :, None, :] * eye[None, :, None, :, None]).reshape(depth, WIDTH_C, WIDTH_C).astype(BF16)
    return w_bf, wpa_p, wpool_bd


def kernel(x_prompt, x_sample, state_a, state_b1, state_b2, state_b3, state_c, norm_w, final_norm_w, w_in, sink_a,
           w_proj_a, w_proj_b, w_proj_c, w_pool, pool_scale, w_out):
    n_batch, seq, _ = x_prompt.shape
    nseq, dec, _ = x_sample.shape
    depth = w_in.shape[0]
    assert dec == DEC

    w_bf, wpa_p, wpool_bd = _prep_weights(w_in, w_proj_a, w_pool)
    wpb, wpc, wout = w_proj_b.astype(BF16), w_proj_c.astype(BF16), w_out.astype(BF16)
    nw3 = norm_w.reshape(depth, 1, D_MODEL)
    ps3 = pool_scale.reshape(depth, 1, WIDTH_C)
    fnw = final_norm_w.reshape(1, D_MODEL)
    tbl_a = jnp.asarray(_tables_a())
    tbl_b = jnp.asarray(_tables_b())
    sl_a, sl_b = (jnp.asarray(t) for t in _decode_slopes())
    sink_prompt = sink_a.reshape(depth, N_KV_A, REP_A).transpose(0, 2, 1).reshape(depth, N_HEADS_A)
    sink_dec = jnp.repeat(sink_prompt, DEC, axis=1)[..., None]

    to_t = lambda s: jnp.transpose(s, (0, 1, 2, 4, 5, 3))
    from_t = lambda s: jnp.transpose(s, (0, 1, 2, 5, 3, 4))
    states = tuple(to_t(s) for s in (state_a, state_b1, state_b2, state_b3))
    sc_t = jnp.transpose(state_c, (0, 2, 1, 3))

    hp = x_prompt.reshape(n_batch * seq, D_MODEL)
    hs = jnp.concatenate([x_sample, x_sample], axis=1).reshape(nseq * DEC_ROWS, D_MODEL)
    new_c_p, new_c_s = [], []
    new_s = None
    out_w = (wpa_p, wpb, wpc, wpool_bd, ps3, wout, fnw)
    stage = _token_stage(n_batch, seq, hp, in_args=(0, nw3, w_bf, depth))
    for l in range(depth):
        final = l == depth - 1
        qkva, qkvb1, qkvb2, qkvb3, gates, uc = stage[:6]
        new_p = stage[6:10]
        new_c_p.append(stage[10])
        qkv_s, gates_s, uc_s, kvt = _inproj_sample(l, hs, nw3, w_bf)
        outs = _decode(l, qkv_s.reshape(nseq, DEC_ROWS, QKV_W), kvt, sink_dec, sl_a, sl_b, states, new_s,
                       sink_prompt[l], qkva.reshape(n_batch, seq, SEG_W), uc.reshape(n_batch, seq, WIDTH_C), tbl_a)
        oa_s, ob_s, new_s, oa, diff = outs[0], outs[1], outs[2:6], outs[6], outs[7]
        ob = _attn_b(*(q.reshape(n_batch, seq, SEG_W) for q in (qkvb1, qkvb2, qkvb3)), tbl_b, n_batch, seq)
        out_args = (l, oa.reshape(-1, WIDTH_A), ob.reshape(-1, WIDTH_B), diff.reshape(-1, WIDTH_C), gates) + out_w
        if final:
            hp = _token_stage(n_batch, seq, hp, out_args=out_args, final=True)[0]
        else:
            res = _token_stage(n_batch, seq, hp, out_args=out_args, in_args=(l + 1, nw3, w_bf, depth), prev_states=new_p)
            hp, stage = res[0], res[1:]
        u_slabs = uc_s.reshape(nseq, DEC_ROWS, WIDTH_C)[:, :DEC].transpose(1, 0, 2)
        diff_t, nc_t = _pool_sample(l, sc_t, u_slabs)
        diff_s = jnp.tile(diff_t.transpose(1, 0, 2), (1, 2, 1)).reshape(nseq * DEC_ROWS, WIDTH_C)
        out_args_s = (l, oa_s.reshape(-1, WIDTH_A), ob_s.reshape(-1, WIDTH_B), diff_s, gates_s) + out_w
        hs = _token_stage(1, nseq * DEC_ROWS, hs, out_args=out_args_s, final=final)[0]
        new_c_s.append(nc_t)

    y_prompt = hp.reshape(n_batch, seq, D_MODEL)
    y_sample = hs.reshape(nseq, DEC_ROWS, D_MODEL)[:, :DEC]
    stack = lambda xs: jnp.stack(xs, axis=0)

    def prompt_state(k, heads, width):
        return from_t(new_p[k].reshape(depth, n_batch, 2, heads, HEAD_DIM, width))

    return (y_prompt, y_sample,
            prompt_state(0, N_KV_A, WINDOW_A), from_t(new_s[0]),
            prompt_state(1, HEADS_B, B_PAIRS[0][0]), from_t(new_s[1]),
            prompt_state(2, HEADS_B, B_PAIRS[1][0]), from_t(new_s[2]),
            prompt_state(3, HEADS_B, B_PAIRS[2][0]), from_t(new_s[3]),
            stack(new_c_p), jnp.transpose(stack(new_c_s), (0, 2, 1, 3)))
```

```python
import functools

import numpy as np
import jax
import jax.numpy as jnp
from jax import lax
from jax.experimental import pallas as pl
from jax.experimental.pallas import tpu as pltpu

F32 = jnp.float32
BF16 = jnp.bfloat16

D_MODEL = 1024
HEAD_DIM = 64
N_HEADS_A = 8
N_KV_A = 2
REP_A = N_HEADS_A // N_KV_A
WINDOW_A = 128
B_PAIRS = ((128, 1), (512, 4), (2048, 16))
N_GROUPS_B = 3
HEADS_B = 4
POOL_WINDOWS = (2, 4, 8, 16)
POOL_STATE = 15
WIDTH_A = N_HEADS_A * HEAD_DIM
WIDTH_KV_A = N_KV_A * HEAD_DIM
WIDTH_B = HEADS_B * HEAD_DIM
WIDTH_C = 256
BLOCK = 128
RMS_EPS = 1e-6
NEG_INF = -1e30
LANES = 128
SUBLANES = 8
VMEM_LIMIT = 56 * 1024 * 1024
DEC = 4
DEC_ROWS = 2 * DEC

COL_QA, COL_KA, COL_VA, COL_GA = 0, 512, 640, 768
COL_QB, COL_KB, COL_VB, COL_GB = 1280, 2048, 2816, 3584
COL_UC, COL_GC, COL_MG = 3840, 4096, 4352
D_IN = 7424
SEG_W = 768
QKV_W = 4 * SEG_W
GATE_W = 4096
KVT_A = 0
KVT_B = (256, 768, 1280)
KVT_ROWS = 1792


def _alibi_slopes(n):
    return 2.0 ** (-8.0 * (np.arange(n, dtype=np.float32) + 1.0) / n)


SLOPES_A = _alibi_slopes(N_HEADS_A)
SLOPES_B = _alibi_slopes(N_GROUPS_B * HEADS_B)


WIN_PREV_CUR = 0
WIN_CUR_NEXT = 1
WIN_MASKED_CUR = 2
N_WIN_KINDS = 3
A_STEP = 2


def _band_table(slope, dist_scale, kind):
    qi = np.arange(BLOCK)[:, None]
    ki = np.arange(2 * BLOCK)[None, :]
    own = (ki < BLOCK) if kind == WIN_CUR_NEXT else (ki >= BLOCK)
    dist = (qi - ki) if kind == WIN_CUR_NEXT else (BLOCK + qi - ki)
    valid = (dist >= 0) & (dist <= BLOCK)
    if kind != WIN_PREV_CUR:
        valid &= own
    bias = -(np.float32(slope) * np.float32(dist_scale)) * dist.astype(np.float32)
    return np.where(valid, bias, np.float32(NEG_INF)).astype(np.float32)


def _tables_a():
    return np.stack([np.stack([_band_table(SLOPES_A[g * REP_A + p], 1, kind) for p in range(REP_A) for g in range(N_KV_A)])
                     for kind in (WIN_PREV_CUR, WIN_CUR_NEXT)])


def _tables_b():
    return np.stack([np.stack([np.stack([_band_table(SLOPES_B[g * HEADS_B + h], B_PAIRS[g][1], kind)
                                         for h in range(HEADS_B)]) for g in range(N_GROUPS_B)])
                     for kind in range(N_WIN_KINDS)])


def _decode_slopes():
    rows_a = np.array([SLOPES_A[(r % DEC_ROWS) // DEC * REP_A + r // DEC_ROWS] for r in range(REP_A * DEC_ROWS)], np.float32)
    rows_b = np.array([[SLOPES_B[g * HEADS_B + r // DEC] for r in range(HEADS_B * DEC)] for g in range(N_GROUPS_B)], np.float32)
    return (np.broadcast_to(rows_a[:, None], (REP_A * DEC_ROWS, LANES)).copy(),
            np.broadcast_to(rows_b[:, :, None], (N_GROUPS_B, HEADS_B * DEC, LANES)).copy())


def _cparams(sem):
    return pltpu.CompilerParams(dimension_semantics=sem, vmem_limit_bytes=VMEM_LIMIT)


def _const_spec(shape):
    nd = len(shape)
    return pl.BlockSpec(shape, lambda *_: (0,) * nd, pipeline_mode=pl.Buffered(1))


def _layer_spec(layer, shape):
    nd = len(shape)
    return pl.BlockSpec((1,) + shape, lambda *_: (layer,) + (0,) * nd, pipeline_mode=pl.Buffered(1))


def _rms_h(x, nw_ref):
    ms = jnp.mean(x * x, axis=-1, keepdims=True)
    return (x * lax.rsqrt(ms + RMS_EPS) * nw_ref[0]).astype(BF16)


def _proj(h, w_ref, lo, width):
    return jnp.dot(h, w_ref[0, :, lo:lo + width], preferred_element_type=F32)


def _store_gates(h, w_ref, gates_ref):
    o = 0
    for lo, width in ((COL_GA, WIDTH_A), (COL_GB, WIDTH_B), (COL_GC, WIDTH_C),
                      (COL_MG, D_MODEL), (COL_MG + D_MODEL, D_MODEL), (COL_MG + 2 * D_MODEL, D_MODEL)):
        gates_ref[:, o:o + width] = _proj(h, w_ref, lo, width).astype(BF16)
        o += width


def _inproj_prompt_tile(x, nw_ref, w_ref, qkva_ref, qkvb1_ref, qkvb2_ref, qkvb3_ref, gates_ref, uc_ref,
                        sa_ref, sb1_ref, sb2_ref, sb3_ref, sc_ref, zscr, tm):
    h = _rms_h(x, nw_ref)

    za = _proj(h, w_ref, COL_QA, SEG_W)
    qkva_ref[...] = za.astype(BF16)
    sa_ref[0, 0, 0] = za[tm - WINDOW_A:, WIDTH_A:WIDTH_A + WIDTH_KV_A].T
    sa_ref[0, 0, 1] = za[tm - WINDOW_A:, WIDTH_A + WIDTH_KV_A:].T

    for g, (s_ref, q_ref) in enumerate(((sb1_ref, qkvb1_ref), (sb2_ref, qkvb2_ref), (sb3_ref, qkvb3_ref))):
        width, dil = B_PAIRS[g]
        keep = min(width, tm)
        parts = [_proj(h, w_ref, col + g * WIDTH_B, WIDTH_B) for col in (COL_QB, COL_KB, COL_VB)]
        s_ref[0, 0, 0] = parts[1][tm - keep:, :].T
        s_ref[0, 0, 1] = parts[2][tm - keep:, :].T
        if dil == 1:
            for k, z in enumerate(parts):
                q_ref[:, k * WIDTH_B:(k + 1) * WIDTH_B] = z.astype(BF16)
            continue
        ncol = WIDTH_B // LANES
        for k, z in enumerate(parts):
            for c in range(ncol):
                zscr[k * ncol + c] = z[:, c * LANES:(c + 1) * LANES]
        for r in range(dil):
            for c in range(SEG_W // LANES):
                q_ref[0, r, :, c * LANES:(c + 1) * LANES] = zscr[c, pl.ds(r, tm // dil, stride=dil), :].astype(BF16)

    _store_gates(h, w_ref, gates_ref)
    zu = _proj(h, w_ref, COL_UC, WIDTH_C)
    uc_ref[...] = zu
    sc_ref[0] = zu[tm - POOL_STATE:, :]


N_OUT_IN = 12
N_INPROJ_OUT = 11


def _token_stage_body(*refs, has_out, has_in, final, n_prev, tm):
    refs = list(refs)
    out_in = refs[:N_OUT_IN] if has_out else refs[:1]
    pos = len(out_in)
    if has_in:
        nw_ref, w_ref = refs[pos:pos + 2]
        pos += 2
    pos += n_prev
    if has_out:
        y_ref = refs[pos]
        pos += 1
        x = _out_tile(*out_in, final=final)
        y_ref[...] = x
    else:
        x = out_in[0][...]
    if has_in:
        _inproj_prompt_tile(x, nw_ref, w_ref, *refs[pos:pos + N_INPROJ_OUT + 1], tm)


def _token_stage(n_batch, seq, x2d, out_args=None, in_args=None, prev_states=None, final=False, tm=256):
    nt = seq // tm
    m = n_batch * seq
    row = lambda n, i: (n * nt + i, 0)
    tile = lambda w: pl.BlockSpec((tm, w), row)
    args, in_specs, out_shape, out_specs = [x2d], [tile(D_MODEL)], [], []
    if out_args is not None:
        lo, oa, ob, diff, gates, wpa, wpb, wpc, wpool, ps, wout, fnw = out_args
        args += [oa, ob, diff, gates, wpa, wpb, wpc, wpool, ps, wout, fnw]
        in_specs += [tile(WIDTH_A), tile(WIDTH_B), tile(WIDTH_C), tile(GATE_W),
                     _layer_spec(lo, (WIDTH_A, D_MODEL)), _layer_spec(lo, (WIDTH_B, D_MODEL)),
                     _layer_spec(lo, (WIDTH_C, D_MODEL)), _layer_spec(lo, (WIDTH_C, WIDTH_C)),
                     _layer_spec(lo, (1, WIDTH_C)), _layer_spec(lo, (D_MODEL, D_MODEL)), _const_spec((1, D_MODEL))]
        out_shape.append(jax.ShapeDtypeStruct((m, D_MODEL), F32))
        out_specs.append(tile(D_MODEL))
    aliases = {}
    scratch = []
    if in_args is not None:
        li, nw, w, depth = in_args
        args += [nw, w]
        in_specs += [_layer_spec(li, (1, D_MODEL)), _layer_spec(li, (D_MODEL, D_IN))]
        b2_tiles = B_PAIRS[1][0] // tm
        state = lambda c, wd: jax.ShapeDtypeStruct((depth, n_batch, 2, c, wd), F32)
        first_state = len(out_shape) + 6
        out_shape += [
            jax.ShapeDtypeStruct((m, SEG_W), BF16),
            jax.ShapeDtypeStruct((m, SEG_W), BF16),
            jax.ShapeDtypeStruct((n_batch, 4, seq // 4, SEG_W), BF16),
            jax.ShapeDtypeStruct((n_batch, 16, seq // 16, SEG_W), BF16),
            jax.ShapeDtypeStruct((m, GATE_W), BF16),
            jax.ShapeDtypeStruct((m, WIDTH_C), F32),
            state(WIDTH_KV_A, WINDOW_A), state(WIDTH_B, B_PAIRS[0][0]),
            state(WIDTH_B, B_PAIRS[1][0]), state(WIDTH_B, B_PAIRS[2][0]),
            jax.ShapeDtypeStruct((n_batch, POOL_STATE, WIDTH_C), F32),
        ]
        out_specs += [
            tile(SEG_W), tile(SEG_W),
            pl.BlockSpec((1, 4, tm // 4, SEG_W), lambda n, i: (n, 0, i, 0)),
            pl.BlockSpec((1, 16, tm // 16, SEG_W), lambda n, i: (n, 0, i, 0)),
            tile(GATE_W), tile(WIDTH_C),
            pl.BlockSpec((1, 1, 2, WIDTH_KV_A, WINDOW_A), lambda n, i: (li, n, 0, 0, 0)),
            pl.BlockSpec((1, 1, 2, WIDTH_B, B_PAIRS[0][0]), lambda n, i: (li, n, 0, 0, 0)),
            pl.BlockSpec((1, 1, 2, WIDTH_B, tm), lambda n, i: (li, n, 0, 0, jnp.maximum(i - (nt - b2_tiles), 0))),
            pl.BlockSpec((1, 1, 2, WIDTH_B, tm), lambda n, i: (li, n, 0, 0, i)),
            pl.BlockSpec((1, POOL_STATE, WIDTH_C), lambda n, i: (n, 0, 0)),
        ]
        scratch = [pltpu.VMEM((SEG_W // LANES, tm, LANES), F32)]
        if prev_states is not None:
            aliases = {len(args) + k: first_state + k for k in range(len(prev_states))}
            in_specs += [pl.BlockSpec(memory_space=pl.ANY)] * len(prev_states)
            args += list(prev_states)
    return pl.pallas_call(
        functools.partial(_token_stage_body, has_out=out_args is not None, has_in=in_args is not None, final=final,
                          n_prev=len(aliases), tm=tm),
        grid=(n_batch, nt),
        in_specs=in_specs,
        out_specs=tuple(out_specs),
        out_shape=tuple(out_shape),
        scratch_shapes=scratch,
        input_output_aliases=aliases,
        compiler_params=_cparams(("arbitrary", "arbitrary")),
        name="token_stage",
    )(*args)


def _inproj_sample_body(x_ref, nw_ref, w_ref, qkv_ref, gates_ref, uc_ref, kvt_ref):
    h = _rms_h(x_ref[...], nw_ref)
    za = _proj(h, w_ref, COL_QA, SEG_W)
    qkv_ref[:, 0:SEG_W] = za
    kvt_ref[KVT_A:KVT_A + WIDTH_KV_A, :] = za[:, WIDTH_A:WIDTH_A + WIDTH_KV_A].T
    kvt_ref[KVT_A + WIDTH_KV_A:KVT_A + 2 * WIDTH_KV_A, :] = za[:, WIDTH_A + WIDTH_KV_A:].T
    for g in range(N_GROUPS_B):
        for k, col in enumerate((COL_QB, COL_KB, COL_VB)):
            o = (1 + g) * SEG_W + k * WIDTH_B
            z = _proj(h, w_ref, col + g * WIDTH_B, WIDTH_B)
            qkv_ref[:, o:o + WIDTH_B] = z
            if k > 0:
                r0 = KVT_B[g] + (k - 1) * WIDTH_B
                kvt_ref[r0:r0 + WIDTH_B, :] = z.T
    _store_gates(h, w_ref, gates_ref)
    uc_ref[...] = _proj(h, w_ref, COL_UC, WIDTH_C)


def _inproj_sample(layer, x2d, nw, w, tm=256):
    m = x2d.shape[0]
    row = lambda i: (i, 0)
    return pl.pallas_call(
        _inproj_sample_body,
        grid=(m // tm,),
        in_specs=[pl.BlockSpec((tm, D_MODEL), row), _layer_spec(layer, (1, D_MODEL)), _layer_spec(layer, (D_MODEL, D_IN))],
        out_specs=(pl.BlockSpec((tm, QKV_W), row), pl.BlockSpec((tm, GATE_W), row), pl.BlockSpec((tm, WIDTH_C), row),
                   pl.BlockSpec((KVT_ROWS, tm), lambda i: (0, i))),
        out_shape=(jax.ShapeDtypeStruct((m, QKV_W), F32), jax.ShapeDtypeStruct((m, GATE_W), BF16),
                   jax.ShapeDtypeStruct((m, WIDTH_C), F32), jax.ShapeDtypeStruct((KVT_ROWS, m), F32)),
        compiler_params=_cparams(("arbitrary",)),
        name="inproj_sample",
    )(x2d, nw, w)


def _lane_masks():
    lane = lax.broadcasted_iota(jnp.int32, (1, LANES), 1)
    lo = lane < HEAD_DIM
    mlo = jnp.where(lo, 1.0, 0.0).astype(BF16)
    mhi = jnp.where(lo, 0.0, 1.0).astype(BF16)
    return lo, mlo, mhi


def _attend(q, kwin, vwin, table, sink=None):
    s = lax.dot_general(q, kwin, (((1,), (1,)), ((), ())), preferred_element_type=F32) + table
    m = jnp.max(s, axis=-1, keepdims=True)
    if sink is not None:
        m = jnp.maximum(m, sink)
    e = jnp.exp(s - m)
    l = jnp.sum(e, axis=-1, keepdims=True)
    if sink is not None:
        l = l + jnp.exp(sink - m)
    acc = jnp.dot(e.astype(BF16), vwin, preferred_element_type=F32)
    return acc * (1.0 / l), m + jnp.log(l)


def _pool_diff(e_ref, rows, t0, pos0):
    cur = e_ref[pl.ds(t0, rows), :]
    acc = cur
    sums = {}
    for k in range(1, POOL_WINDOWS[-1]):
        acc = acc + e_ref[pl.ds(t0 - k, rows), :]
        if k + 1 in POOL_WINDOWS:
            sums[k + 1] = acc
    lane = lax.broadcasted_iota(jnp.int32, (rows, WIDTH_C), 1)
    pos = lax.broadcasted_iota(jnp.int32, (rows, WIDTH_C), 0) + pos0
    wsum = sums[POOL_WINDOWS[-1]]
    win = jnp.full((rows, WIDTH_C), POOL_WINDOWS[-1], jnp.int32)
    for gi in range(len(POOL_WINDOWS) - 2, -1, -1):
        sel = lane < (gi + 1) * HEAD_DIM
        wsum = jnp.where(sel, sums[POOL_WINDOWS[gi]], wsum)
        win = jnp.where(sel, POOL_WINDOWS[gi], win)
    cnt = jnp.minimum(pos + 1, win).astype(F32)
    return wsum / cnt - cur


def _attn_a_block(i, sink_ref, qkv_ref, tbl_ref, oa_ref, out_row):
    lo, mlo, mhi = _lane_masks()
    kc = WIDTH_A
    vc = WIDTH_A + WIDTH_KV_A
    first = (i == 0).astype(jnp.int32)
    kind = first * WIN_CUR_NEXT
    r0 = pl.multiple_of(i * BLOCK, BLOCK)
    w0 = pl.multiple_of((i - 1 + first) * BLOCK, BLOCK)
    kwin = qkv_ref[0, pl.ds(w0, 2 * BLOCK), kc:kc + WIDTH_KV_A]
    vwin = qkv_ref[0, pl.ds(w0, 2 * BLOCK), vc:vc + WIDTH_KV_A]
    for p in range(REP_A):
        qp = qkv_ref[0, pl.ds(r0, BLOCK), p * LANES:(p + 1) * LANES]
        o_lo, _ = _attend(qp * mlo, kwin, vwin, tbl_ref[kind, 2 * p], sink_ref[2 * p])
        o_hi, _ = _attend(qp * mhi, kwin, vwin, tbl_ref[kind, 2 * p + 1], sink_ref[2 * p + 1])
        oa_ref[0, out_row:out_row + BLOCK, p * LANES:(p + 1) * LANES] = jnp.where(lo, o_lo, o_hi)


def _pool_sequence(uc_ref, diff_ref, e_ref, seq):
    pad = 2 * SUBLANES
    e_ref[0:pad, :] = jnp.zeros((pad, WIDTH_C), F32)
    e_ref[pad:, :] = uc_ref[0]
    rows = 256
    for c in range(seq // rows):
        diff_ref[0, c * rows:(c + 1) * rows, :] = _pool_diff(e_ref, rows, pad + c * rows, c * rows)


def _attn_b_body(q1_ref, q2_ref, q3_ref, tbl_ref, ob_ref, o_nat, l_nat, o_stage, l_stage, *, seq):
    lo, mlo, mhi = _lane_masks()
    npair = WIDTH_B // LANES

    def block(q_ref, r0, w0, table, dst_o, dst_l):
        for p in range(npair):
            qp = q_ref[0, pl.ds(r0, BLOCK), p * LANES:(p + 1) * LANES]
            kwin = q_ref[0, pl.ds(w0, 2 * BLOCK), WIDTH_B + p * LANES:WIDTH_B + (p + 1) * LANES]
            vwin = q_ref[0, pl.ds(w0, 2 * BLOCK), 2 * WIDTH_B + p * LANES:2 * WIDTH_B + (p + 1) * LANES]
            o_lo, l_lo = _attend(qp * mlo, kwin, vwin, table(2 * p))
            o_hi, l_hi = _attend(qp * mhi, kwin, vwin, table(2 * p + 1))
            dst_o[p, pl.ds(r0, BLOCK), :] = jnp.where(lo, o_lo, o_hi)
            dst_l[p, pl.ds(r0, BLOCK), :] = jnp.where(lo, l_lo, l_hi)

    table = lambda kind, g: (lambda u: tbl_ref[kind, g, u])

    block(q1_ref, 0, 0, table(WIN_CUR_NEXT, 0), o_nat.at[0], l_nat.at[0])

    def loop1(i, c):
        r0 = pl.multiple_of(i * BLOCK, BLOCK)
        block(q1_ref, r0, r0 - BLOCK, table(WIN_PREV_CUR, 0), o_nat.at[0], l_nat.at[0])
        return c

    lax.fori_loop(1, seq // BLOCK, loop1, 0, unroll=True)

    sub = seq // B_PAIRS[1][1]
    for r in range(B_PAIRS[1][1]):
        block(q2_ref, r * sub, r * sub, table(WIN_CUR_NEXT, 1), o_stage, l_stage)

    def loop2(i, c):
        for r in range(B_PAIRS[1][1]):
            r0 = pl.multiple_of(r * sub + i * BLOCK, BLOCK)
            block(q2_ref, r0, r0 - BLOCK, table(WIN_PREV_CUR, 1), o_stage, l_stage)
        return c

    lax.fori_loop(1, sub // BLOCK, loop2, 0, unroll=True)
    for r in range(B_PAIRS[1][1]):
        for p in range(npair):
            o_nat[1, p, pl.ds(r, sub, stride=B_PAIRS[1][1]), :] = o_stage[p, r * sub:(r + 1) * sub, :]
            l_nat[1, p, pl.ds(r, sub, stride=B_PAIRS[1][1]), :] = l_stage[p, r * sub:(r + 1) * sub, :]

    dil3 = B_PAIRS[2][1]
    par = 8

    def loop3(rg, c):
        for k in range(par):
            r = rg * par + k
            last = r // (dil3 - 1)
            r0 = pl.multiple_of(r * BLOCK, BLOCK)
            w0 = pl.multiple_of((r - last) * BLOCK, BLOCK)
            block(q3_ref, r0, w0, (lambda u, last=last: tbl_ref[WIN_CUR_NEXT + last, 2, u]), o_stage, l_stage)
        return c

    lax.fori_loop(0, dil3 // par, loop3, 0)
    for r in range(dil3):
        for p in range(npair):
            o_nat[2, p, pl.ds(r, BLOCK, stride=dil3), :] = o_stage[p, r * BLOCK:(r + 1) * BLOCK, :]
            l_nat[2, p, pl.ds(r, BLOCK, stride=dil3), :] = l_stage[p, r * BLOCK:(r + 1) * BLOCK, :]

    rows = 256
    for c in range(seq // rows):
        for p in range(npair):
            ls = [l_nat[g, p, c * rows:(c + 1) * rows, :] for g in range(N_GROUPS_B)]
            mx = jnp.maximum(jnp.maximum(ls[0], ls[1]), ls[2])
            ws = [jnp.exp(x - mx) for x in ls]
            den = ws[0] + ws[1] + ws[2]
            num = sum(ws[g] * o_nat[g, p, c * rows:(c + 1) * rows, :] for g in range(N_GROUPS_B))
            ob_ref[0, c * rows:(c + 1) * rows, p * LANES:(p + 1) * LANES] = num * (1.0 / den)


def _attn_b(q1, q2, q3, tbl, n_batch, seq):
    npair = WIDTH_B // LANES
    return pl.pallas_call(
        functools.partial(_attn_b_body, seq=seq),
        grid=(n_batch,),
        in_specs=[pl.BlockSpec((1, seq, SEG_W), lambda n: (n, 0, 0))] * N_GROUPS_B
                 + [_const_spec((N_WIN_KINDS, N_GROUPS_B, HEADS_B, BLOCK, 2 * BLOCK))],
        out_specs=pl.BlockSpec((1, seq, WIDTH_B), lambda n: (n, 0, 0)),
        out_shape=jax.ShapeDtypeStruct((n_batch, seq, WIDTH_B), F32),
        scratch_shapes=[pltpu.VMEM((N_GROUPS_B, npair, seq, LANES), F32),
                        pltpu.VMEM((N_GROUPS_B, npair, seq, LANES), F32),
                        pltpu.VMEM((npair, seq, LANES), F32),
                        pltpu.VMEM((npair, seq, LANES), F32)],
        compiler_params=_cparams(("arbitrary",)),
        name="attn_b",
    )(q1, q2, q3, tbl)


def _sigmoid(x):
    return 0.5 * jnp.tanh(0.5 * x) + 0.5


def _out_tile(x_ref, oa_ref, ob_ref, diff_ref, g_ref, wpa_ref, wpb_ref, wpc_ref, wpool_ref, ps_ref, wout_ref,
              fnw_ref, *, final):
    def gate(lo, hi):
        return g_ref[:, lo:hi].astype(F32)

    def silu(g):
        return g * _sigmoid(g)

    o = WIDTH_A
    ga, gb, gc = gate(0, o), gate(o, o + WIDTH_B), gate(o + WIDTH_B, o + WIDTH_B + WIDTH_C)
    mbase = o + WIDTH_B + WIDTH_C
    pa = jnp.dot((oa_ref[...] * silu(ga)).astype(BF16), wpa_ref[0], preferred_element_type=F32)
    pb = jnp.dot((ob_ref[...] * silu(gb)).astype(BF16), wpb_ref[0], preferred_element_type=F32)
    oc = jnp.dot(diff_ref[...].astype(BF16), wpool_ref[0], preferred_element_type=F32) * ps_ref[0]
    pc = jnp.dot((oc * silu(gc)).astype(BF16), wpc_ref[0], preferred_element_type=F32)
    m = (_sigmoid(gate(mbase, mbase + D_MODEL)) * pa
         + _sigmoid(gate(mbase + D_MODEL, mbase + 2 * D_MODEL)) * pb
         + _sigmoid(gate(mbase + 2 * D_MODEL, mbase + 3 * D_MODEL)) * pc)
    y = x_ref[...] + jnp.dot(m.astype(BF16), wout_ref[0], preferred_element_type=F32)
    if final:
        ms = jnp.mean(y * y, axis=-1, keepdims=True)
        y = y * lax.rsqrt(ms + RMS_EPS) * fnw_ref[...]
    return y


def _decode_body(*refs, has_prev, seq):
    n_dec = 9
    n_prm = 4
    psink_ref, pqkv_ref, puc_ref, ptbl_ref = refs[n_dec:n_dec + n_prm]
    outs = refs[n_dec + n_prm + (4 if has_prev else 0):]
    poa_ref, pdiff_ref, e_ref = outs[6:]
    i = pl.program_id(1)
    _decode_sequence(pl.program_id(0) * (seq // BLOCK) + i, refs[:n_dec], outs[:6])

    @pl.when(i % A_STEP == 0)
    def _():
        for k in range(A_STEP):
            _attn_a_block(i + k, psink_ref, pqkv_ref, ptbl_ref, poa_ref, k * BLOCK)

    @pl.when(i == 0)
    def _():
        _pool_sequence(puc_ref, pdiff_ref, e_ref, seq)


def _decode_sequence(n, in_refs, out_refs):
    qkv_ref, kvt_ref, sink_ref, sla_ref, slb_ref, sa_ref, sb1_ref, sb2_ref, sb3_ref = in_refs
    oa_ref, ob_ref, na_ref, nb1_ref, nb2_ref, nb3_ref = out_refs
    new_shift = (LANES - DEC) - (n % (LANES // DEC_ROWS)) * DEC_ROWS

    def shifted_state(s_ref, n_ref, kv, rows, width, kvt_row):
        x = s_ref[0, 0, kv].reshape(rows, width)
        rolled = pltpu.roll(x, width - DEC, axis=1)
        new_t = pltpu.roll(kvt_ref[kvt_row:kvt_row + rows, :], new_shift, axis=1)
        lane = lax.broadcasted_iota(jnp.int32, (rows, LANES), 1)
        last = jnp.where(lane < LANES - DEC, rolled[:, width - LANES:], new_t)
        heads = rows // HEAD_DIM
        if width > LANES:
            n_ref[0, 0, kv, :, :, 0:width - LANES] = rolled[:, 0:width - LANES].reshape(heads, HEAD_DIM, width - LANES)
        n_ref[0, 0, kv, :, :, width - LANES:width] = last.reshape(heads, HEAD_DIM, LANES)
        return x

    def attend(q, kt, vt, knew, vnew, live, slopes, width, dil, sink):
        rows = q.shape[0]
        r_i = lax.broadcasted_iota(jnp.int32, (rows, width), 0)
        t_i = lax.broadcasted_iota(jnp.int32, (rows, width), 1)
        jq = r_i & (DEC - 1)
        s = jnp.dot(q.astype(BF16), kt.astype(BF16), preferred_element_type=F32)
        dist = (float(width) + jq.astype(F32)) - t_i.astype(F32)
        valid = (t_i >= jq) if dil == 1 else ((t_i & (dil - 1)) == jq)
        s = jnp.where(valid, s - slopes * dist, NEG_INF)
        jq1 = jq[:, :1].astype(F32)
        s_new = []
        if dil == 1:
            for j2 in range(DEC):
                kb = jnp.broadcast_to(knew[j2:j2 + 1, :], q.shape)
                d2 = jq1 - float(j2)
                s_new.append(jnp.where(d2 >= 0.0, jnp.sum(q * kb, axis=-1, keepdims=True) - slopes * d2, NEG_INF))
        else:
            s_new.append(jnp.sum(q * knew, axis=-1, keepdims=True))
        m = jnp.max(s, axis=-1, keepdims=True)
        for sn in s_new:
            m = jnp.maximum(m, sn)
        if sink is not None:
            m = jnp.maximum(m, sink)
        e = jnp.exp(s - m)
        l = jnp.sum(e, axis=-1, keepdims=True)
        acc = lax.dot_general(e.astype(BF16), vt.astype(BF16), (((1,), (1,)), ((), ())), preferred_element_type=F32)
        for j2, sn in enumerate(s_new):
            en = jnp.exp(sn - m)
            l = l + en
            vb = jnp.broadcast_to(vnew[j2:j2 + 1, :], q.shape) if dil == 1 else vnew
            acc = acc + en * vb
        if sink is not None:
            l = l + jnp.exp(sink - m)
        return jnp.where(live, acc * (1.0 / l), 0.0), m + jnp.log(l)

    kt = shifted_state(sa_ref, na_ref, 0, WIDTH_KV_A, WINDOW_A, KVT_A)
    vt = shifted_state(sa_ref, na_ref, 1, WIDTH_KV_A, WINDOW_A, KVT_A + WIDTH_KV_A)
    ra = REP_A * DEC_ROWS
    row = lax.broadcasted_iota(jnp.int32, (ra, LANES), 0)
    lane = lax.broadcasted_iota(jnp.int32, (ra, LANES), 1)
    live_a = ((row & (DEC_ROWS - 1)) < DEC) == (lane < HEAD_DIM)
    q32 = jnp.concatenate([qkv_ref[0, :, p * LANES:(p + 1) * LANES] for p in range(REP_A)], axis=0)
    q32 = jnp.where(live_a, q32, 0.0)
    knew = jnp.concatenate([qkv_ref[0, :, WIDTH_A:WIDTH_A + WIDTH_KV_A]] * REP_A, axis=0)
    vnew = jnp.concatenate([qkv_ref[0, :, WIDTH_A + WIDTH_KV_A:SEG_W]] * REP_A, axis=0)
    o32, _ = attend(q32, kt, vt, knew, vnew, live_a, sla_ref[:, 0:1], WINDOW_A, 1, sink_ref[0])
    for p in range(REP_A):
        r8 = o32[p * DEC_ROWS:(p + 1) * DEC_ROWS]
        oa_ref[0, :, p * LANES:(p + 1) * LANES] = r8 + pltpu.roll(r8, DEC, axis=0)

    rb = HEADS_B * DEC
    row = lax.broadcasted_iota(jnp.int32, (rb, WIDTH_B), 0)
    lane = lax.broadcasted_iota(jnp.int32, (rb, WIDTH_B), 1)
    live_b = (lane // HEAD_DIM) == (row // DEC)
    o_g, lse_g = [], []
    for g, (s_ref, n_ref) in enumerate(((sb1_ref, nb1_ref), (sb2_ref, nb2_ref), (sb3_ref, nb3_ref))):
        width, dil = B_PAIRS[g]
        base = (1 + g) * SEG_W
        kt = shifted_state(s_ref, n_ref, 0, WIDTH_B, width, KVT_B[g])
        vt = shifted_state(s_ref, n_ref, 1, WIDTH_B, width, KVT_B[g] + WIDTH_B)
        q16 = jnp.where(live_b, jnp.concatenate([qkv_ref[0, :, base:base + WIDTH_B]] * 2, axis=0), 0.0)
        knew = jnp.concatenate([qkv_ref[0, :, base + WIDTH_B:base + 2 * WIDTH_B]] * 2, axis=0)
        vnew = jnp.concatenate([qkv_ref[0, :, base + 2 * WIDTH_B:base + 3 * WIDTH_B]] * 2, axis=0)
        o16, lse = attend(q16, kt, vt, knew, vnew, live_b, slb_ref[g, :, 0:1], width, dil, None)
        o_g.append(o16)
        lse_g.append(lse)
    mx = jnp.maximum(jnp.maximum(lse_g[0], lse_g[1]), lse_g[2])
    ws = [jnp.exp(x - mx) for x in lse_g]
    o16 = (ws[0] * o_g[0] + ws[1] * o_g[1] + ws[2] * o_g[2]) * (1.0 / (ws[0] + ws[1] + ws[2]))
    x8 = o16[0:DEC_ROWS] + o16[DEC_ROWS:]
    ob_ref[0] = x8 + pltpu.roll(x8, DEC, axis=0)


def _decode(layer, qkv8, kvt, sink32, slopes_a, slopes_b, states, prev, p_sink, p_qkva, p_uc, p_tbl):
    nseq = qkv8.shape[0]
    n_batch, seq = p_qkva.shape[:2]
    nblk = seq // BLOCK
    assert nseq == n_batch * nblk
    dseq = lambda b, i: b * nblk + i
    st = lambda a: pl.BlockSpec((1, 1) + a.shape[2:], lambda b, i: (layer, dseq(b, i), 0, 0, 0, 0))
    per_seq = lambda w: pl.BlockSpec((1, DEC_ROWS, w), lambda b, i: (dseq(b, i), 0, 0))
    whole = lambda w: pl.BlockSpec((1, seq, w), lambda b, i: (b, 0, 0))
    in_specs = [per_seq(QKV_W),
                pl.BlockSpec((KVT_ROWS, LANES), lambda b, i: (0, dseq(b, i) // (LANES // DEC_ROWS))),
                _layer_spec(layer, sink32.shape[1:]), _const_spec(slopes_a.shape), _const_spec(slopes_b.shape)]
    in_specs += [st(a) for a in states]
    in_specs += [pl.BlockSpec(memory_space=pltpu.SMEM), whole(SEG_W), whole(WIDTH_C), _const_spec(p_tbl.shape)]
    args = [qkv8, kvt, sink32, slopes_a, slopes_b, *states, p_sink, p_qkva, p_uc, p_tbl]
    aliases = {}
    if prev is not None:
        in_specs += [pl.BlockSpec(memory_space=pl.ANY)] * len(prev)
        aliases = {len(args) + k: 2 + k for k in range(len(prev))}
        args += list(prev)
    out_specs = ((per_seq(WIDTH_A), per_seq(WIDTH_B)) + tuple(st(a) for a in states)
                 + (pl.BlockSpec((1, A_STEP * BLOCK, WIDTH_A), lambda b, i: (b, i // A_STEP, 0)), whole(WIDTH_C)))
    out_shape = ((jax.ShapeDtypeStruct((nseq, DEC_ROWS, WIDTH_A), F32), jax.ShapeDtypeStruct((nseq, DEC_ROWS, WIDTH_B), F32))
                 + tuple(jax.ShapeDtypeStruct(a.shape, F32) for a in states)
                 + (jax.ShapeDtypeStruct((n_batch, seq, WIDTH_A), F32), jax.ShapeDtypeStruct((n_batch, seq, WIDTH_C), F32)))
    return pl.pallas_call(
        functools.partial(_decode_body, has_prev=prev is not None, seq=seq),
        grid=(n_batch, nblk),
        in_specs=in_specs,
        out_specs=out_specs,
        out_shape=out_shape,
        scratch_shapes=[pltpu.VMEM((seq + 2 * SUBLANES, WIDTH_C), F32)],
        input_output_aliases=aliases,
        compiler_params=_cparams(("arbitrary", "arbitrary")),
        name="decode_attn",
    )(*args)


def _pool_sample_body(sc_ref, u_ref, diff_ref, nc_ref):
    rows = [sc_ref[0, k] for k in range(POOL_STATE)] + [u_ref[j] for j in range(DEC)]
    lane = lax.broadcasted_iota(jnp.int32, rows[0].shape, 1)
    for j in range(DEC):
        t = POOL_STATE + j
        acc = rows[t]
        sums = {}
        for k in range(1, POOL_WINDOWS[-1]):
            acc = acc + rows[t - k]
            if k + 1 in POOL_WINDOWS:
                sums[k + 1] = acc
        mean = sums[POOL_WINDOWS[-1]] / float(POOL_WINDOWS[-1])
        for gi in range(len(POOL_WINDOWS) - 2, -1, -1):
            mean = jnp.where(lane < (gi + 1) * HEAD_DIM, sums[POOL_WINDOWS[gi]] / float(POOL_WINDOWS[gi]), mean)
        diff_ref[j] = mean - rows[t]
    for k in range(POOL_STATE):
        nc_ref[k] = rows[k + DEC]


def _pool_sample(layer, sc_t, u_slabs):
    nseq = u_slabs.shape[1]
    return pl.pallas_call(
        _pool_sample_body,
        grid=(1,),
        in_specs=[pl.BlockSpec((1, POOL_STATE, nseq, WIDTH_C), lambda i: (layer, 0, 0, 0)),
                  pl.BlockSpec((DEC, nseq, WIDTH_C), lambda i: (0, 0, 0))],
        out_specs=(pl.BlockSpec((DEC, nseq, WIDTH_C), lambda i: (0, 0, 0)),
                   pl.BlockSpec((POOL_STATE, nseq, WIDTH_C), lambda i: (0, 0, 0))),
        out_shape=(jax.ShapeDtypeStruct((DEC, nseq, WIDTH_C), F32), jax.ShapeDtypeStruct((POOL_STATE, nseq, WIDTH_C), F32)),
        compiler_params=_cparams(("arbitrary",)),
        name="pool_sample",
    )(sc_t, u_slabs)


def _head_pairs(x, axis):
    shp = x.shape
    x = x.reshape(shp[:axis] + (N_KV_A, REP_A, HEAD_DIM) + shp[axis + 1:])
    x = jnp.swapaxes(x, axis, axis + 1)
    return x.reshape(shp)


def _prep_weights(w_in, w_proj_a, w_pool):
    scale = HEAD_DIM ** -0.5
    pieces = [
        (_head_pairs(w_in[:, :, COL_QA:COL_KA], 2) * scale).astype(BF16),
        w_in[:, :, COL_KA:COL_GA].astype(BF16),
        _head_pairs(w_in[:, :, COL_GA:COL_QB], 2).astype(BF16),
        (w_in[:, :, COL_QB:COL_KB] * scale).astype(BF16),
        w_in[:, :, COL_KB:].astype(BF16),
    ]
    w_bf = jnp.concatenate(pieces, axis=-1)
    wpa_p = _head_pairs(w_proj_a, 1).astype(BF16)
    depth = w_pool.shape[0]
    ng = len(POOL_WINDOWS)
    eye = jnp.eye(ng, dtype=w_pool.dtype)
    wpool_bd = (w_pool[:, :, :, None, :] * eye[None, :, None, :, None]).reshape(depth, WIDTH_C, WIDTH_C).astype(BF16)
    return w_bf, wpa_p, wpool_bd


def kernel(x_prompt, x_sample, state_a, state_b1, state_b2, state_b3, state_c, norm_w, final_norm_w, w_in, sink_a,
           w_proj_a, w_proj_b, w_proj_c, w_pool, pool_scale, w_out):
    n_batch, seq, _ = x_prompt.shape
    nseq, dec, _ = x_sample.shape
    depth = w_in.shape[0]
    assert dec == DEC

    w_bf, wpa_p, wpool_bd = _prep_weights(w_in, w_proj_a, w_pool)
    wpb, wpc, wout = w_proj_b.astype(BF16), w_proj_c.astype(BF16), w_out.astype(BF16)
    nw3 = norm_w.reshape(depth, 1, D_MODEL)
    ps3 = pool_scale.reshape(depth, 1, WIDTH_C)
    fnw = final_norm_w.reshape(1, D_MODEL)
    tbl_a = jnp.asarray(_tables_a())
    tbl_b = jnp.asarray(_tables_b())
    sl_a, sl_b = (jnp.asarray(t) for t in _decode_slopes())
    sink_prompt = sink_a.reshape(depth, N_KV_A, REP_A).transpose(0, 2, 1).reshape(depth, N_HEADS_A)
    sink_dec = jnp.repeat(sink_prompt, DEC, axis=1)[..., None]

    to_t = lambda s: jnp.transpose(s, (0, 1, 2, 4, 5, 3))
    from_t = lambda s: jnp.transpose(s, (0, 1, 2, 5, 3, 4))
    states = tuple(to_t(s) for s in (state_a, state_b1, state_b2, state_b3))
    sc_t = jnp.transpose(state_c, (0, 2, 1, 3))

    hp = x_prompt.reshape(n_batch * seq, D_MODEL)
    hs = jnp.concatenate([x_sample, x_sample], axis=1).reshape(nseq * DEC_ROWS, D_MODEL)
    new_c_p, new_c_s = [], []
    new_s = None
    out_w = (wpa_p, wpb, wpc, wpool_bd, ps3, wout, fnw)
    stage = _token_stage(n_batch, seq, hp, in_args=(0, nw3, w_bf, depth))
    for l in range(depth):
        final = l == depth - 1
        qkva, qkvb1, qkvb2, qkvb3, gates, uc = stage[:6]
        new_p = stage[6:10]
        new_c_p.append(stage[10])
        qkv_s, gates_s, uc_s, kvt = _inproj_sample(l, hs, nw3, w_bf)
        outs = _decode(l, qkv_s.reshape(nseq, DEC_ROWS, QKV_W), kvt, sink_dec, sl_a, sl_b, states, new_s,
                       sink_prompt[l], qkva.reshape(n_batch, seq, SEG_W), uc.reshape(n_batch, seq, WIDTH_C), tbl_a)
        oa_s, ob_s, new_s, oa, diff = outs[0], outs[1], outs[2:6], outs[6], outs[7]
        ob = _attn_b(*(q.reshape(n_batch, seq, SEG_W) for q in (qkvb1, qkvb2, qkvb3)), tbl_b, n_batch, seq)
        out_args = (l, oa.reshape(-1, WIDTH_A), ob.reshape(-1, WIDTH_B), diff.reshape(-1, WIDTH_C), gates) + out_w
        if final:
            hp = _token_stage(n_batch, seq, hp, out_args=out_args, final=True)[0]
        else:
            res = _token_stage(n_batch, seq, hp, out_args=out_args, in_args=(l + 1, nw3, w_bf, depth), prev_states=new_p)
            hp, stage = res[0], res[1:]
        u_slabs = uc_s.reshape(nseq, DEC_ROWS, WIDTH_C)[:, :DEC].transpose(1, 0, 2)
        diff_t, nc_t = _pool_sample(l, sc_t, u_slabs)
        diff_s = jnp.tile(diff_t.transpose(1, 0, 2), (1, 2, 1)).reshape(nseq * DEC_ROWS, WIDTH_C)
        out_args_s = (l, oa_s.reshape(-1, WIDTH_A), ob_s.reshape(-1, WIDTH_B), diff_s, gates_s) + out_w
        hs = _token_stage(1, nseq * DEC_ROWS, hs, out_args=out_args_s, final=final)[0]
        new_c_s.append(nc_t)

    y_prompt = hp.reshape(n_batch, seq, D_MODEL)
    y_sample = hs.reshape(nseq, DEC_ROWS, D_MODEL)[:, :DEC]
    stack = lambda xs: jnp.stack(xs, axis=0)

    def prompt_state(k, heads, width):
        return from_t(new_p[k].reshape(depth, n_batch, 2, heads, HEAD_DIM, width))

    return (y_prompt, y_sample,
            prompt_state(0, N_KV_A, WINDOW_A), from_t(new_s[0]),
            prompt_state(1, HEADS_B, B_PAIRS[0][0]), from_t(new_s[1]),
            prompt_state(2, HEADS_B, B_PAIRS[1][0]), from_t(new_s[2]),
            prompt_state(3, HEADS_B, B_PAIRS[2][0]), from_t(new_s[3]),
            stack(new_c_p), jnp.transpose(stack(new_c_s), (0, 2, 1, 3)))
```

```python
import functools

import numpy as np
import jax
import jax.numpy as jnp
from jax import lax
from jax.experimental import pallas as pl
from jax.experimental.pallas import tpu as pltpu

F32 = jnp.float32
BF16 = jnp.bfloat16

D_MODEL = 1024
HEAD_DIM = 64
N_HEADS_A = 8
N_KV_A = 2
REP_A = N_HEADS_A // N_KV_A
WINDOW_A = 128
B_PAIRS = ((128, 1), (512, 4), (2048, 16))
N_GROUPS_B = 3
HEADS_B = 4
POOL_WINDOWS = (2, 4, 8, 16)
POOL_STATE = 15
WIDTH_A = N_HEADS_A * HEAD_DIM
WIDTH_KV_A = N_KV_A * HEAD_DIM
WIDTH_B = HEADS_B * HEAD_DIM
WIDTH_C = 256
BLOCK = 128
RMS_EPS = 1e-6
NEG_INF = -1e30
LANES = 128
SUBLANES = 8
VMEM_LIMIT = 56 * 1024 * 1024
DEC = 4
DEC_ROWS = 2 * DEC

COL_QA, COL_KA, COL_VA, COL_GA = 0, 512, 640, 768
COL_QB, COL_KB, COL_VB, COL_GB = 1280, 2048, 2816, 3584
COL_UC, COL_GC, COL_MG = 3840, 4096, 4352
D_IN = 7424
SEG_W = 768
QKV_W = 4 * SEG_W
GATE_W = 4096
KVT_A = 0
KVT_B = (256, 768, 1280)
KVT_ROWS = 1792


def _alibi_slopes(n):
    return 2.0 ** (-8.0 * (np.arange(n, dtype=np.float32) + 1.0) / n)


SLOPES_A = _alibi_slopes(N_HEADS_A)
SLOPES_B = _alibi_slopes(N_GROUPS_B * HEADS_B)


WIN_PREV_CUR = 0
WIN_CUR_NEXT = 1
WIN_MASKED_CUR = 2
N_WIN_KINDS = 3


def _band_table(slope, dist_scale, kind):
    qi = np.arange(BLOCK)[:, None]
    ki = np.arange(2 * BLOCK)[None, :]
    own = (ki < BLOCK) if kind == WIN_CUR_NEXT else (ki >= BLOCK)
    dist = (qi - ki) if kind == WIN_CUR_NEXT else (BLOCK + qi - ki)
    valid = (dist >= 0) & (dist <= BLOCK)
    if kind != WIN_PREV_CUR:
        valid &= own
    bias = -(np.float32(slope) * np.float32(dist_scale)) * dist.astype(np.float32)
    return np.where(valid, bias, np.float32(NEG_INF)).astype(np.float32)


def _tables_a():
    return np.stack([np.stack([_band_table(SLOPES_A[g * REP_A + p], 1, kind) for p in range(REP_A) for g in range(N_KV_A)])
                     for kind in (WIN_PREV_CUR, WIN_CUR_NEXT)])


def _tables_b():
    return np.stack([np.stack([np.stack([_band_table(SLOPES_B[g * HEADS_B + h], B_PAIRS[g][1], kind)
                                         for h in range(HEADS_B)]) for g in range(N_GROUPS_B)])
                     for kind in range(N_WIN_KINDS)])


def _decode_slopes():
    rows_a = np.array([SLOPES_A[(r % DEC_ROWS) // DEC * REP_A + r // DEC_ROWS] for r in range(REP_A * DEC_ROWS)], np.float32)
    rows_b = np.array([[SLOPES_B[g * HEADS_B + r // DEC] for r in range(HEADS_B * DEC)] for g in range(N_GROUPS_B)], np.float32)
    return (np.broadcast_to(rows_a[:, None], (REP_A * DEC_ROWS, LANES)).copy(),
            np.broadcast_to(rows_b[:, :, None], (N_GROUPS_B, HEADS_B * DEC, LANES)).copy())


def _cparams(sem):
    return pltpu.CompilerParams(dimension_semantics=sem, vmem_limit_bytes=VMEM_LIMIT)


def _const_spec(shape):
    nd = len(shape)
    return pl.BlockSpec(shape, lambda *_: (0,) * nd, pipeline_mode=pl.Buffered(1))


def _layer_spec(layer, shape):
    nd = len(shape)
    return pl.BlockSpec((1,) + shape, lambda *_: (layer,) + (0,) * nd, pipeline_mode=pl.Buffered(1))


def _rms_h(x, nw_ref):
    ms = jnp.mean(x * x, axis=-1, keepdims=True)
    return (x * lax.rsqrt(ms + RMS_EPS) * nw_ref[0]).astype(BF16)


def _proj(h, w_ref, lo, width):
    return jnp.dot(h, w_ref[0, :, lo:lo + width], preferred_element_type=F32)


def _store_gates(h, w_ref, gates_ref):
    o = 0
    for lo, width in ((COL_GA, WIDTH_A), (COL_GB, WIDTH_B), (COL_GC, WIDTH_C),
                      (COL_MG, D_MODEL), (COL_MG + D_MODEL, D_MODEL), (COL_MG + 2 * D_MODEL, D_MODEL)):
        gates_ref[:, o:o + width] = _proj(h, w_ref, lo, width).astype(BF16)
        o += width


def _inproj_prompt_tile(x, nw_ref, w_ref, qkva_ref, qkvb1_ref, qkvb2_ref, qkvb3_ref, gates_ref, uc_ref,
                        sa_ref, sb1_ref, sb2_ref, sb3_ref, sc_ref, zscr, tm):
    h = _rms_h(x, nw_ref)

    za = _proj(h, w_ref, COL_QA, SEG_W)
    qkva_ref[...] = za.astype(BF16)
    sa_ref[0, 0, 0] = za[tm - WINDOW_A:, WIDTH_A:WIDTH_A + WIDTH_KV_A].T
    sa_ref[0, 0, 1] = za[tm - WINDOW_A:, WIDTH_A + WIDTH_KV_A:].T

    for g, (s_ref, q_ref) in enumerate(((sb1_ref, qkvb1_ref), (sb2_ref, qkvb2_ref), (sb3_ref, qkvb3_ref))):
        width, dil = B_PAIRS[g]
        keep = min(width, tm)
        parts = [_proj(h, w_ref, col + g * WIDTH_B, WIDTH_B) for col in (COL_QB, COL_KB, COL_VB)]
        s_ref[0, 0, 0] = parts[1][tm - keep:, :].T
        s_ref[0, 0, 1] = parts[2][tm - keep:, :].T
        if dil == 1:
            for k, z in enumerate(parts):
                q_ref[:, k * WIDTH_B:(k + 1) * WIDTH_B] = z.astype(BF16)
            continue
        ncol = WIDTH_B // LANES
        for k, z in enumerate(parts):
            for c in range(ncol):
                zscr[k * ncol + c] = z[:, c * LANES:(c + 1) * LANES]
        for r in range(dil):
            for c in range(SEG_W // LANES):
                q_ref[0, r, :, c * LANES:(c + 1) * LANES] = zscr[c, pl.ds(r, tm // dil, stride=dil), :].astype(BF16)

    _store_gates(h, w_ref, gates_ref)
    zu = _proj(h, w_ref, COL_UC, WIDTH_C)
    uc_ref[...] = zu
    sc_ref[0] = zu[tm - POOL_STATE:, :]


N_OUT_IN = 12
N_INPROJ_OUT = 11


def _token_stage_body(*refs, has_out, has_in, final, n_prev, tm):
    refs = list(refs)
    out_in = refs[:N_OUT_IN] if has_out else refs[:1]
    pos = len(out_in)
    if has_in:
        nw_ref, w_ref = refs[pos:pos + 2]
        pos += 2
    pos += n_prev
    if has_out:
        y_ref = refs[pos]
        pos += 1
        x = _out_tile(*out_in, final=final)
        y_ref[...] = x
    else:
        x = out_in[0][...]
    if has_in:
        _inproj_prompt_tile(x, nw_ref, w_ref, *refs[pos:pos + N_INPROJ_OUT + 1], tm)


def _token_stage(n_batch, seq, x2d, out_args=None, in_args=None, prev_states=None, final=False, tm=256):
    nt = seq // tm
    m = n_batch * seq
    row = lambda n, i: (n * nt + i, 0)
    tile = lambda w: pl.BlockSpec((tm, w), row)
    args, in_specs, out_shape, out_specs = [x2d], [tile(D_MODEL)], [], []
    if out_args is not None:
        lo, oa, ob, diff, gates, wpa, wpb, wpc, wpool, ps, wout, fnw = out_args
        args += [oa, ob, diff, gates, wpa, wpb, wpc, wpool, ps, wout, fnw]
        in_specs += [tile(WIDTH_A), tile(WIDTH_B), tile(WIDTH_C), tile(GATE_W),
                     _layer_spec(lo, (WIDTH_A, D_MODEL)), _layer_spec(lo, (WIDTH_B, D_MODEL)),
                     _layer_spec(lo, (WIDTH_C, D_MODEL)), _layer_spec(lo, (WIDTH_C, WIDTH_C)),
                     _layer_spec(lo, (1, WIDTH_C)), _layer_spec(lo, (D_MODEL, D_MODEL)), _const_spec((1, D_MODEL))]
        out_shape.append(jax.ShapeDtypeStruct((m, D_MODEL), F32))
        out_specs.append(tile(D_MODEL))
    aliases = {}
    scratch = []
    if in_args is not None:
        li, nw, w, depth = in_args
        args += [nw, w]
        in_specs += [_layer_spec(li, (1, D_MODEL)), _layer_spec(li, (D_MODEL, D_IN))]
        b2_tiles = B_PAIRS[1][0] // tm
        state = lambda c, wd: jax.ShapeDtypeStruct((depth, n_batch, 2, c, wd), F32)
        first_state = len(out_shape) + 6
        out_shape += [
            jax.ShapeDtypeStruct((m, SEG_W), BF16),
            jax.ShapeDtypeStruct((m, SEG_W), BF16),
            jax.ShapeDtypeStruct((n_batch, 4, seq // 4, SEG_W), BF16),
            jax.ShapeDtypeStruct((n_batch, 16, seq // 16, SEG_W), BF16),
            jax.ShapeDtypeStruct((m, GATE_W), BF16),
            jax.ShapeDtypeStruct((m, WIDTH_C), F32),
            state(WIDTH_KV_A, WINDOW_A), state(WIDTH_B, B_PAIRS[0][0]),
            state(WIDTH_B, B_PAIRS[1][0]), state(WIDTH_B, B_PAIRS[2][0]),
            jax.ShapeDtypeStruct((n_batch, POOL_STATE, WIDTH_C), F32),
        ]
        out_specs += [
            tile(SEG_W), tile(SEG_W),
            pl.BlockSpec((1, 4, tm // 4, SEG_W), lambda n, i: (n, 0, i, 0)),
            pl.BlockSpec((1, 16, tm // 16, SEG_W), lambda n, i: (n, 0, i, 0)),
            tile(GATE_W), tile(WIDTH_C),
            pl.BlockSpec((1, 1, 2, WIDTH_KV_A, WINDOW_A), lambda n, i: (li, n, 0, 0, 0)),
            pl.BlockSpec((1, 1, 2, WIDTH_B, B_PAIRS[0][0]), lambda n, i: (li, n, 0, 0, 0)),
            pl.BlockSpec((1, 1, 2, WIDTH_B, tm), lambda n, i: (li, n, 0, 0, jnp.maximum(i - (nt - b2_tiles), 0))),
            pl.BlockSpec((1, 1, 2, WIDTH_B, tm), lambda n, i: (li, n, 0, 0, i)),
            pl.BlockSpec((1, POOL_STATE, WIDTH_C), lambda n, i: (n, 0, 0)),
        ]
        scratch = [pltpu.VMEM((SEG_W // LANES, tm, LANES), F32)]
        if prev_states is not None:
            aliases = {len(args) + k: first_state + k for k in range(len(prev_states))}
            in_specs += [pl.BlockSpec(memory_space=pl.ANY)] * len(prev_states)
            args += list(prev_states)
    return pl.pallas_call(
        functools.partial(_token_stage_body, has_out=out_args is not None, has_in=in_args is not None, final=final,
                          n_prev=len(aliases), tm=tm),
        grid=(n_batch, nt),
        in_specs=in_specs,
        out_specs=tuple(out_specs),
        out_shape=tuple(out_shape),
        scratch_shapes=scratch,
        input_output_aliases=aliases,
        compiler_params=_cparams(("arbitrary", "arbitrary")),
        name="token_stage",
    )(*args)


def _inproj_sample_body(x_ref, nw_ref, w_ref, qkv_ref, gates_ref, uc_ref, kvt_ref):
    h = _rms_h(x_ref[...], nw_ref)
    za = _proj(h, w_ref, COL_QA, SEG_W)
    qkv_ref[:, 0:SEG_W] = za
    kvt_ref[KVT_A:KVT_A + WIDTH_KV_A, :] = za[:, WIDTH_A:WIDTH_A + WIDTH_KV_A].T
    kvt_ref[KVT_A + WIDTH_KV_A:KVT_A + 2 * WIDTH_KV_A, :] = za[:, WIDTH_A + WIDTH_KV_A:].T
    for g in range(N_GROUPS_B):
        for k, col in enumerate((COL_QB, COL_KB, COL_VB)):
            o = (1 + g) * SEG_W + k * WIDTH_B
            z = _proj(h, w_ref, col + g * WIDTH_B, WIDTH_B)
            qkv_ref[:, o:o + WIDTH_B] = z
            if k > 0:
                r0 = KVT_B[g] + (k - 1) * WIDTH_B
                kvt_ref[r0:r0 + WIDTH_B, :] = z.T
    _store_gates(h, w_ref, gates_ref)
    uc_ref[...] = _proj(h, w_ref, COL_UC, WIDTH_C)


def _inproj_sample(layer, x2d, nw, w, tm=256):
    m = x2d.shape[0]
    row = lambda i: (i, 0)
    return pl.pallas_call(
        _inproj_sample_body,
        grid=(m // tm,),
        in_specs=[pl.BlockSpec((tm, D_MODEL), row), _layer_spec(layer, (1, D_MODEL)), _layer_spec(layer, (D_MODEL, D_IN))],
        out_specs=(pl.BlockSpec((tm, QKV_W), row), pl.BlockSpec((tm, GATE_W), row), pl.BlockSpec((tm, WIDTH_C), row),
                   pl.BlockSpec((KVT_ROWS, tm), lambda i: (0, i))),
        out_shape=(jax.ShapeDtypeStruct((m, QKV_W), F32), jax.ShapeDtypeStruct((m, GATE_W), BF16),
                   jax.ShapeDtypeStruct((m, WIDTH_C), F32), jax.ShapeDtypeStruct((KVT_ROWS, m), F32)),
        compiler_params=_cparams(("arbitrary",)),
        name="inproj_sample",
    )(x2d, nw, w)


def _lane_masks():
    lane = lax.broadcasted_iota(jnp.int32, (1, LANES), 1)
    lo = lane < HEAD_DIM
    mlo = jnp.where(lo, 1.0, 0.0).astype(BF16)
    mhi = jnp.where(lo, 0.0, 1.0).astype(BF16)
    return lo, mlo, mhi


def _attend(q, kwin, vwin, table, sink=None):
    s = lax.dot_general(q, kwin, (((1,), (1,)), ((), ())), preferred_element_type=F32) + table
    m = jnp.max(s, axis=-1, keepdims=True)
    if sink is not None:
        m = jnp.maximum(m, sink)
    e = jnp.exp(s - m)
    l = jnp.sum(e, axis=-1, keepdims=True)
    if sink is not None:
        l = l + jnp.exp(sink - m)
    acc = jnp.dot(e.astype(BF16), vwin, preferred_element_type=F32)
    return acc * (1.0 / l), m + jnp.log(l)


def _pool_diff(e_ref, rows, t0, pos0):
    cur = e_ref[pl.ds(t0, rows), :]
    acc = cur
    sums = {}
    for k in range(1, POOL_WINDOWS[-1]):
        acc = acc + e_ref[pl.ds(t0 - k, rows), :]
        if k + 1 in POOL_WINDOWS:
            sums[k + 1] = acc
    lane = lax.broadcasted_iota(jnp.int32, (rows, WIDTH_C), 1)
    pos = lax.broadcasted_iota(jnp.int32, (rows, WIDTH_C), 0) + pos0
    wsum = sums[POOL_WINDOWS[-1]]
    win = jnp.full((rows, WIDTH_C), POOL_WINDOWS[-1], jnp.int32)
    for gi in range(len(POOL_WINDOWS) - 2, -1, -1):
        sel = lane < (gi + 1) * HEAD_DIM
        wsum = jnp.where(sel, sums[POOL_WINDOWS[gi]], wsum)
        win = jnp.where(sel, POOL_WINDOWS[gi], win)
    cnt = jnp.minimum(pos + 1, win).astype(F32)
    return wsum / cnt - cur


def _attn_a_block(i, sink_ref, qkv_ref, tbl_ref, oa_ref):
    lo, mlo, mhi = _lane_masks()
    kc = WIDTH_A
    vc = WIDTH_A + WIDTH_KV_A
    first = (i == 0).astype(jnp.int32)
    kind = first * WIN_CUR_NEXT
    r0 = pl.multiple_of(i * BLOCK, BLOCK)
    w0 = pl.multiple_of((i - 1 + first) * BLOCK, BLOCK)
    kwin = qkv_ref[0, pl.ds(w0, 2 * BLOCK), kc:kc + WIDTH_KV_A]
    vwin = qkv_ref[0, pl.ds(w0, 2 * BLOCK), vc:vc + WIDTH_KV_A]
    for p in range(REP_A):
        qp = qkv_ref[0, pl.ds(r0, BLOCK), p * LANES:(p + 1) * LANES]
        o_lo, _ = _attend(qp * mlo, kwin, vwin, tbl_ref[kind, 2 * p], sink_ref[2 * p])
        o_hi, _ = _attend(qp * mhi, kwin, vwin, tbl_ref[kind, 2 * p + 1], sink_ref[2 * p + 1])
        oa_ref[0, :, p * LANES:(p + 1) * LANES] = jnp.where(lo, o_lo, o_hi)


def _pool_sequence(uc_ref, diff_ref, e_ref, seq):
    pad = 2 * SUBLANES
    e_ref[0:pad, :] = jnp.zeros((pad, WIDTH_C), F32)
    e_ref[pad:, :] = uc_ref[0]
    rows = 256
    for c in range(seq // rows):
        diff_ref[0, c * rows:(c + 1) * rows, :] = _pool_diff(e_ref, rows, pad + c * rows, c * rows)


def _attn_b_body(q1_ref, q2_ref, q3_ref, tbl_ref, ob_ref, o_nat, l_nat, o_stage, l_stage, *, seq):
    lo, mlo, mhi = _lane_masks()
    npair = WIDTH_B // LANES

    def block(q_ref, r0, w0, table, dst_o, dst_l):
        for p in range(npair):
            qp = q_ref[0, pl.ds(r0, BLOCK), p * LANES:(p + 1) * LANES]
            kwin = q_ref[0, pl.ds(w0, 2 * BLOCK), WIDTH_B + p * LANES:WIDTH_B + (p + 1) * LANES]
            vwin = q_ref[0, pl.ds(w0, 2 * BLOCK), 2 * WIDTH_B + p * LANES:2 * WIDTH_B + (p + 1) * LANES]
            o_lo, l_lo = _attend(qp * mlo, kwin, vwin, table(2 * p))
            o_hi, l_hi = _attend(qp * mhi, kwin, vwin, table(2 * p + 1))
            dst_o[p, pl.ds(r0, BLOCK), :] = jnp.where(lo, o_lo, o_hi)
            dst_l[p, pl.ds(r0, BLOCK), :] = jnp.where(lo, l_lo, l_hi)

    table = lambda kind, g: (lambda u: tbl_ref[kind, g, u])

    block(q1_ref, 0, 0, table(WIN_CUR_NEXT, 0), o_nat.at[0], l_nat.at[0])

    def loop1(i, c):
        r0 = pl.multiple_of(i * BLOCK, BLOCK)
        block(q1_ref, r0, r0 - BLOCK, table(WIN_PREV_CUR, 0), o_nat.at[0], l_nat.at[0])
        return c

    lax.fori_loop(1, seq // BLOCK, loop1, 0, unroll=True)

    sub = seq // B_PAIRS[1][1]
    for r in range(B_PAIRS[1][1]):
        block(q2_ref, r * sub, r * sub, table(WIN_CUR_NEXT, 1), o_stage, l_stage)

    def loop2(i, c):
        for r in range(B_PAIRS[1][1]):
            r0 = pl.multiple_of(r * sub + i * BLOCK, BLOCK)
            block(q2_ref, r0, r0 - BLOCK, table(WIN_PREV_CUR, 1), o_stage, l_stage)
        return c

    lax.fori_loop(1, sub // BLOCK, loop2, 0, unroll=True)
    for r in range(B_PAIRS[1][1]):
        for p in range(npair):
            o_nat[1, p, pl.ds(r, sub, stride=B_PAIRS[1][1]), :] = o_stage[p, r * sub:(r + 1) * sub, :]
            l_nat[1, p, pl.ds(r, sub, stride=B_PAIRS[1][1]), :] = l_stage[p, r * sub:(r + 1) * sub, :]

    dil3 = B_PAIRS[2][1]
    par = 8

    def loop3(rg, c):
        for k in range(par):
            r = rg * par + k
            last = r // (dil3 - 1)
            r0 = pl.multiple_of(r * BLOCK, BLOCK)
            w0 = pl.multiple_of((r - last) * BLOCK, BLOCK)
            block(q3_ref, r0, w0, (lambda u, last=last: tbl_ref[WIN_CUR_NEXT + last, 2, u]), o_stage, l_stage)
        return c

    lax.fori_loop(0, dil3 // par, loop3, 0)
    for r in range(dil3):
        for p in range(npair):
            o_nat[2, p, pl.ds(r, BLOCK, stride=dil3), :] = o_stage[p, r * BLOCK:(r + 1) * BLOCK, :]
            l_nat[2, p, pl.ds(r, BLOCK, stride=dil3), :] = l_stage[p, r * BLOCK:(r + 1) * BLOCK, :]

    rows = 256
    for c in range(seq // rows):
        for p in range(npair):
            ls = [l_nat[g, p, c * rows:(c + 1) * rows, :] for g in range(N_GROUPS_B)]
            mx = jnp.maximum(jnp.maximum(ls[0], ls[1]), ls[2])
            ws = [jnp.exp(x - mx) for x in ls]
            den = ws[0] + ws[1] + ws[2]
            num = sum(ws[g] * o_nat[g, p, c * rows:(c + 1) * rows, :] for g in range(N_GROUPS_B))
            ob_ref[0, c * rows:(c + 1) * rows, p * LANES:(p + 1) * LANES] = num * (1.0 / den)


def _attn_b(q1, q2, q3, tbl, n_batch, seq):
    npair = WIDTH_B // LANES
    return pl.pallas_call(
        functools.partial(_attn_b_body, seq=seq),
        grid=(n_batch,),
        in_specs=[pl.BlockSpec((1, seq, SEG_W), lambda n: (n, 0, 0))] * N_GROUPS_B
                 + [_const_spec((N_WIN_KINDS, N_GROUPS_B, HEADS_B, BLOCK, 2 * BLOCK))],
        out_specs=pl.BlockSpec((1, seq, WIDTH_B), lambda n: (n, 0, 0)),
        out_shape=jax.ShapeDtypeStruct((n_batch, seq, WIDTH_B), F32),
        scratch_shapes=[pltpu.VMEM((N_GROUPS_B, npair, seq, LANES), F32),
                        pltpu.VMEM((N_GROUPS_B, npair, seq, LANES), F32),
                        pltpu.VMEM((npair, seq, LANES), F32),
                        pltpu.VMEM((npair, seq, LANES), F32)],
        compiler_params=_cparams(("arbitrary",)),
        name="attn_b",
    )(q1, q2, q3, tbl)


def _sigmoid(x):
    return 0.5 * jnp.tanh(0.5 * x) + 0.5


def _out_tile(x_ref, oa_ref, ob_ref, diff_ref, g_ref, wpa_ref, wpb_ref, wpc_ref, wpool_ref, ps_ref, wout_ref,
              fnw_ref, *, final):
    def gate(lo, hi):
        return g_ref[:, lo:hi].astype(F32)

    def silu(g):
        return g * _sigmoid(g)

    o = WIDTH_A
    ga, gb, gc = gate(0, o), gate(o, o + WIDTH_B), gate(o + WIDTH_B, o + WIDTH_B + WIDTH_C)
    mbase = o + WIDTH_B + WIDTH_C
    pa = jnp.dot((oa_ref[...] * silu(ga)).astype(BF16), wpa_ref[0], preferred_element_type=F32)
    pb = jnp.dot((ob_ref[...] * silu(gb)).astype(BF16), wpb_ref[0], preferred_element_type=F32)
    oc = jnp.dot(diff_ref[...].astype(BF16), wpool_ref[0], preferred_element_type=F32) * ps_ref[0]
    pc = jnp.dot((oc * silu(gc)).astype(BF16), wpc_ref[0], preferred_element_type=F32)
    m = (_sigmoid(gate(mbase, mbase + D_MODEL)) * pa
         + _sigmoid(gate(mbase + D_MODEL, mbase + 2 * D_MODEL)) * pb
         + _sigmoid(gate(mbase + 2 * D_MODEL, mbase + 3 * D_MODEL)) * pc)
    y = x_ref[...] + jnp.dot(m.astype(BF16), wout_ref[0], preferred_element_type=F32)
    if final:
        ms = jnp.mean(y * y, axis=-1, keepdims=True)
        y = y * lax.rsqrt(ms + RMS_EPS) * fnw_ref[...]
    return y


def _decode_body(*refs, has_prev, seq):
    n_dec = 9
    n_prm = 4
    psink_ref, pqkv_ref, puc_ref, ptbl_ref = refs[n_dec:n_dec + n_prm]
    outs = refs[n_dec + n_prm + (4 if has_prev else 0):]
    poa_ref, pdiff_ref, e_ref = outs[6:]
    i = pl.program_id(1)
    _decode_sequence(pl.program_id(0) * (seq // BLOCK) + i, refs[:n_dec], outs[:6])

    @pl.when(i >= 0)
    def _():
        _attn_a_block(i, psink_ref, pqkv_ref, ptbl_ref, poa_ref)

    @pl.when(i == 0)
    def _():
        _pool_sequence(puc_ref, pdiff_ref, e_ref, seq)


def _decode_sequence(n, in_refs, out_refs):
    qkv_ref, kvt_ref, sink_ref, sla_ref, slb_ref, sa_ref, sb1_ref, sb2_ref, sb3_ref = in_refs
    oa_ref, ob_ref, na_ref, nb1_ref, nb2_ref, nb3_ref = out_refs
    new_shift = (LANES - DEC) - (n % (LANES // DEC_ROWS)) * DEC_ROWS

    def shifted_state(s_ref, n_ref, kv, rows, width, kvt_row):
        x = s_ref[0, 0, kv].reshape(rows, width)
        rolled = pltpu.roll(x, width - DEC, axis=1)
        new_t = pltpu.roll(kvt_ref[kvt_row:kvt_row + rows, :], new_shift, axis=1)
        lane = lax.broadcasted_iota(jnp.int32, (rows, LANES), 1)
        last = jnp.where(lane < LANES - DEC, rolled[:, width - LANES:], new_t)
        heads = rows // HEAD_DIM
        if width > LANES:
            n_ref[0, 0, kv, :, :, 0:width - LANES] = rolled[:, 0:width - LANES].reshape(heads, HEAD_DIM, width - LANES)
        n_ref[0, 0, kv, :, :, width - LANES:width] = last.reshape(heads, HEAD_DIM, LANES)
        return x

    def attend(q, kt, vt, knew, vnew, live, slopes, width, dil, sink):
        rows = q.shape[0]
        r_i = lax.broadcasted_iota(jnp.int32, (rows, width), 0)
        t_i = lax.broadcasted_iota(jnp.int32, (rows, width), 1)
        jq = r_i & (DEC - 1)
        s = jnp.dot(q.astype(BF16), kt.astype(BF16), preferred_element_type=F32)
        dist = (float(width) + jq.astype(F32)) - t_i.astype(F32)
        valid = (t_i >= jq) if dil == 1 else ((t_i & (dil - 1)) == jq)
        s = jnp.where(valid, s - slopes * dist, NEG_INF)
        jq1 = jq[:, :1].astype(F32)
        s_new = []
        if dil == 1:
            for j2 in range(DEC):
                kb = jnp.broadcast_to(knew[j2:j2 + 1, :], q.shape)
                d2 = jq1 - float(j2)
                s_new.append(jnp.where(d2 >= 0.0, jnp.sum(q * kb, axis=-1, keepdims=True) - slopes * d2, NEG_INF))
        else:
            s_new.append(jnp.sum(q * knew, axis=-1, keepdims=True))
        m = jnp.max(s, axis=-1, keepdims=True)
        for sn in s_new:
            m = jnp.maximum(m, sn)
        if sink is not None:
            m = jnp.maximum(m, sink)
        e = jnp.exp(s - m)
        l = jnp.sum(e, axis=-1, keepdims=True)
        acc = lax.dot_general(e.astype(BF16), vt.astype(BF16), (((1,), (1,)), ((), ())), preferred_element_type=F32)
        for j2, sn in enumerate(s_new):
            en = jnp.exp(sn - m)
            l = l + en
            vb = jnp.broadcast_to(vnew[j2:j2 + 1, :], q.shape) if dil == 1 else vnew
            acc = acc + en * vb
        if sink is not None:
            l = l + jnp.exp(sink - m)
        return jnp.where(live, acc * (1.0 / l), 0.0), m + jnp.log(l)

    kt = shifted_state(sa_ref, na_ref, 0, WIDTH_KV_A, WINDOW_A, KVT_A)
    vt = shifted_state(sa_ref, na_ref, 1, WIDTH_KV_A, WINDOW_A, KVT_A + WIDTH_KV_A)
    ra = REP_A * DEC_ROWS
    row = lax.broadcasted_iota(jnp.int32, (ra, LANES), 0)
    lane = lax.broadcasted_iota(jnp.int32, (ra, LANES), 1)
    live_a = ((row & (DEC_ROWS - 1)) < DEC) == (lane < HEAD_DIM)
    q32 = jnp.concatenate([qkv_ref[0, :, p * LANES:(p + 1) * LANES] for p in range(REP_A)], axis=0)
    q32 = jnp.where(live_a, q32, 0.0)
    knew = jnp.concatenate([qkv_ref[0, :, WIDTH_A:WIDTH_A + WIDTH_KV_A]] * REP_A, axis=0)
    vnew = jnp.concatenate([qkv_ref[0, :, WIDTH_A + WIDTH_KV_A:SEG_W]] * REP_A, axis=0)
    o32, _ = attend(q32, kt, vt, knew, vnew, live_a, sla_ref[:, 0:1], WINDOW_A, 1, sink_ref[0])
    for p in range(REP_A):
        r8 = o32[p * DEC_ROWS:(p + 1) * DEC_ROWS]
        oa_ref[0, :, p * LANES:(p + 1) * LANES] = r8 + pltpu.roll(r8, DEC, axis=0)

    rb = HEADS_B * DEC
    row = lax.broadcasted_iota(jnp.int32, (rb, WIDTH_B), 0)
    lane = lax.broadcasted_iota(jnp.int32, (rb, WIDTH_B), 1)
    live_b = (lane // HEAD_DIM) == (row // DEC)
    o_g, lse_g = [], []
    for g, (s_ref, n_ref) in enumerate(((sb1_ref, nb1_ref), (sb2_ref, nb2_ref), (sb3_ref, nb3_ref))):
        width, dil = B_PAIRS[g]
        base = (1 + g) * SEG_W
        kt = shifted_state(s_ref, n_ref, 0, WIDTH_B, width, KVT_B[g])
        vt = shifted_state(s_ref, n_ref, 1, WIDTH_B, width, KVT_B[g] + WIDTH_B)
        q16 = jnp.where(live_b, jnp.concatenate([qkv_ref[0, :, base:base + WIDTH_B]] * 2, axis=0), 0.0)
        knew = jnp.concatenate([qkv_ref[0, :, base + WIDTH_B:base + 2 * WIDTH_B]] * 2, axis=0)
        vnew = jnp.concatenate([qkv_ref[0, :, base + 2 * WIDTH_B:base + 3 * WIDTH_B]] * 2, axis=0)
        o16, lse = attend(q16, kt, vt, knew, vnew, live_b, slb_ref[g, :, 0:1], width, dil, None)
        o_g.append(o16)
        lse_g.append(lse)
    mx = jnp.maximum(jnp.maximum(lse_g[0], lse_g[1]), lse_g[2])
    ws = [jnp.exp(x - mx) for x in lse_g]
    o16 = (ws[0] * o_g[0] + ws[1] * o_g[1] + ws[2] * o_g[2]) * (1.0 / (ws[0] + ws[1] + ws[2]))
    x8 = o16[0:DEC_ROWS] + o16[DEC_ROWS:]
    ob_ref[0] = x8 + pltpu.roll(x8, DEC, axis=0)


def _decode(layer, qkv8, kvt, sink32, slopes_a, slopes_b, states, prev, p_sink, p_qkva, p_uc, p_tbl):
    nseq = qkv8.shape[0]
    n_batch, seq = p_qkva.shape[:2]
    nblk = seq // BLOCK
    assert nseq == n_batch * nblk
    dseq = lambda b, i: b * nblk + i
    st = lambda a: pl.BlockSpec((1, 1) + a.shape[2:], lambda b, i: (layer, dseq(b, i), 0, 0, 0, 0))
    per_seq = lambda w: pl.BlockSpec((1, DEC_ROWS, w), lambda b, i: (dseq(b, i), 0, 0))
    whole = lambda w: pl.BlockSpec((1, seq, w), lambda b, i: (b, 0, 0))
    in_specs = [per_seq(QKV_W),
                pl.BlockSpec((KVT_ROWS, LANES), lambda b, i: (0, dseq(b, i) // (LANES // DEC_ROWS))),
                _layer_spec(layer, sink32.shape[1:]), _const_spec(slopes_a.shape), _const_spec(slopes_b.shape)]
    in_specs += [st(a) for a in states]
    in_specs += [pl.BlockSpec(memory_space=pltpu.SMEM), whole(SEG_W), whole(WIDTH_C), _const_spec(p_tbl.shape)]
    args = [qkv8, kvt, sink32, slopes_a, slopes_b, *states, p_sink, p_qkva, p_uc, p_tbl]
    aliases = {}
    if prev is not None:
        in_specs += [pl.BlockSpec(memory_space=pl.ANY)] * len(prev)
        aliases = {len(args) + k: 2 + k for k in range(len(prev))}
        args += list(prev)
    out_specs = ((per_seq(WIDTH_A), per_seq(WIDTH_B)) + tuple(st(a) for a in states)
                 + (pl.BlockSpec((1, BLOCK, WIDTH_A), lambda b, i: (b, i, 0)), whole(WIDTH_C)))
    out_shape = ((jax.ShapeDtypeStruct((nseq, DEC_ROWS, WIDTH_A), F32), jax.ShapeDtypeStruct((nseq, DEC_ROWS, WIDTH_B), F32))
                 + tuple(jax.ShapeDtypeStruct(a.shape, F32) for a in states)
                 + (jax.ShapeDtypeStruct((n_batch, seq, WIDTH_A), F32), jax.ShapeDtypeStruct((n_batch, seq, WIDTH_C), F32)))
    return pl.pallas_call(
        functools.partial(_decode_body, has_prev=prev is not None, seq=seq),
        grid=(n_batch, nblk),
        in_specs=in_specs,
        out_specs=out_specs,
        out_shape=out_shape,
        scratch_shapes=[pltpu.VMEM((seq + 2 * SUBLANES, WIDTH_C), F32)],
        input_output_aliases=aliases,
        compiler_params=_cparams(("arbitrary", "arbitrary")),
        name="decode_attn",
    )(*args)


def _pool_sample_body(sc_ref, u_ref, diff_ref, nc_ref):
    rows = [sc_ref[0, k] for k in range(POOL_STATE)] + [u_ref[j] for j in range(DEC)]
    lane = lax.broadcasted_iota(jnp.int32, rows[0].shape, 1)
    for j in range(DEC):
        t = POOL_STATE + j
        acc = rows[t]
        sums = {}
        for k in range(1, POOL_WINDOWS[-1]):
            acc = acc + rows[t - k]
            if k + 1 in POOL_WINDOWS:
                sums[k + 1] = acc
        mean = sums[POOL_WINDOWS[-1]] / float(POOL_WINDOWS[-1])
        for gi in range(len(POOL_WINDOWS) - 2, -1, -1):
            mean = jnp.where(lane < (gi + 1) * HEAD_DIM, sums[POOL_WINDOWS[gi]] / float(POOL_WINDOWS[gi]), mean)
        diff_ref[j] = mean - rows[t]
    for k in range(POOL_STATE):
        nc_ref[k] = rows[k + DEC]


def _pool_sample(layer, sc_t, u_slabs):
    nseq = u_slabs.shape[1]
    return pl.pallas_call(
        _pool_sample_body,
        grid=(1,),
        in_specs=[pl.BlockSpec((1, POOL_STATE, nseq, WIDTH_C), lambda i: (layer, 0, 0, 0)),
                  pl.BlockSpec((DEC, nseq, WIDTH_C), lambda i: (0, 0, 0))],
        out_specs=(pl.BlockSpec((DEC, nseq, WIDTH_C), lambda i: (0, 0, 0)),
                   pl.BlockSpec((POOL_STATE, nseq, WIDTH_C), lambda i: (0, 0, 0))),
        out_shape=(jax.ShapeDtypeStruct((DEC, nseq, WIDTH_C), F32), jax.ShapeDtypeStruct((POOL_STATE, nseq, WIDTH_C), F32)),
        compiler_params=_cparams(("arbitrary",)),
        name="pool_sample",
    )(sc_t, u_slabs)


def _head_pairs(x, axis):
    shp = x.shape
    x = x.reshape(shp[:axis] + (N_KV_A, REP_A, HEAD_DIM) + shp[axis + 1:])
    x = jnp.swapaxes(x, axis, axis + 1)
    return x.reshape(shp)


def _prep_weights(w_in, w_proj_a, w_pool):
    scale = HEAD_DIM ** -0.5
    pieces = [
        (_head_pairs(w_in[:, :, COL_QA:COL_KA], 2) * scale).astype(BF16),
        w_in[:, :, COL_KA:COL_GA].astype(BF16),
        _head_pairs(w_in[:, :, COL_GA:COL_QB], 2).astype(BF16),
        (w_in[:, :, COL_QB:COL_KB] * scale).astype(BF16),
        w_in[:, :, COL_KB:].astype(BF16),
    ]
    w_bf = jnp.concatenate(pieces, axis=-1)
    wpa_p = _head_pairs(w_proj_a, 1).astype(BF16)
    depth = w_pool.shape[0]
    ng = len(POOL_WINDOWS)
    eye = jnp.eye(ng, dtype=w_pool.dtype)
    wpool_bd = (w_pool[:, :, :, None, :] * eye[None, :, None, :, None]).reshape(depth, WIDTH_C, WIDTH_C).astype(BF16)
    return w_bf, wpa_p, wpool_bd


def kernel(x_prompt, x_sample, state_a, state_b1, state_b2, state_b3, state_c, norm_w, final_norm_w, w_in, sink_a,
           w_proj_a, w_proj_b, w_proj_c, w_pool, pool_scale, w_out):
    n_batch, seq, _ = x_prompt.shape
    nseq, dec, _ = x_sample.shape
    depth = w_in.shape[0]
    assert dec == DEC

    w_bf, wpa_p, wpool_bd = _prep_weights(w_in, w_proj_a, w_pool)
    wpb, wpc, wout = w_proj_b.astype(BF16), w_proj_c.astype(BF16), w_out.astype(BF16)
    nw3 = norm_w.reshape(depth, 1, D_MODEL)
    ps3 = pool_scale.reshape(depth, 1, WIDTH_C)
    fnw = final_norm_w.reshape(1, D_MODEL)
    tbl_a = jnp.asarray(_tables_a())
    tbl_b = jnp.asarray(_tables_b())
    sl_a, sl_b = (jnp.asarray(t) for t in _decode_slopes())
    sink_prompt = sink_a.reshape(depth, N_KV_A, REP_A).transpose(0, 2, 1).reshape(depth, N_HEADS_A)
    sink_dec = jnp.repeat(sink_prompt, DEC, axis=1)[..., None]

    to_t = lambda s: jnp.transpose(s, (0, 1, 2, 4, 5, 3))
    from_t = lambda s: jnp.transpose(s, (0, 1, 2, 5, 3, 4))
    states = tuple(to_t(s) for s in (state_a, state_b1, state_b2, state_b3))
    sc_t = jnp.transpose(state_c, (0, 2, 1, 3))

    hp = x_prompt.reshape(n_batch * seq, D_MODEL)
    hs = jnp.concatenate([x_sample, x_sample], axis=1).reshape(nseq * DEC_ROWS, D_MODEL)
    new_c_p, new_c_s = [], []
    new_s = None
    out_w = (wpa_p, wpb, wpc, wpool_bd, ps3, wout, fnw)
    stage = _token_stage(n_batch, seq, hp, in_args=(0, nw3, w_bf, depth))
    for l in range(depth):
        final = l == depth - 1
        qkva, qkvb1, qkvb2, qkvb3, gates, uc = stage[:6]
        new_p = stage[6:10]
        new_c_p.append(stage[10])
        qkv_s, gates_s, uc_s, kvt = _inproj_sample(l, hs, nw3, w_bf)
        outs = _decode(l, qkv_s.reshape(nseq, DEC_ROWS, QKV_W), kvt, sink_dec, sl_a, sl_b, states, new_s,
                       sink_prompt[l], qkva.reshape(n_batch, seq, SEG_W), uc.reshape(n_batch, seq, WIDTH_C), tbl_a)
        oa_s, ob_s, new_s, oa, diff = outs[0], outs[1], outs[2:6], outs[6], outs[7]
        ob = _attn_b(*(q.reshape(n_batch, seq, SEG_W) for q in (qkvb1, qkvb2, qkvb3)), tbl_b, n_batch, seq)
        out_args = (l, oa.reshape(-1, WIDTH_A), ob.reshape(-1, WIDTH_B), diff.reshape(-1, WIDTH_C), gates) + out_w
        if final:
            hp = _token_stage(n_batch, seq, hp, out_args=out_args, final=True)[0]
        else:
            res = _token_stage(n_batch, seq, hp, out_args=out_args, in_args=(l + 1, nw3, w_bf, depth), prev_states=new_p)
            hp, stage = res[0], res[1:]
        u_slabs = uc_s.reshape(nseq, DEC_ROWS, WIDTH_C)[:, :DEC].transpose(1, 0, 2)
        diff_t, nc_t = _pool_sample(l, sc_t, u_slabs)
        diff_s = jnp.tile(diff_t.transpose(1, 0, 2), (1, 2, 1)).reshape(nseq * DEC_ROWS, WIDTH_C)
        out_args_s = (l, oa_s.reshape(-1, WIDTH_A), ob_s.reshape(-1, WIDTH_B), diff_s, gates_s) + out_w
        hs = _token_stage(1, nseq * DEC_ROWS, hs, out_args=out_args_s, final=final)[0]
        new_c_s.append(nc_t)

    y_prompt = hp.reshape(n_batch, seq, D_MODEL)
    y_sample = hs.reshape(nseq, DEC_ROWS, D_MODEL)[:, :DEC]
    stack = lambda xs: jnp.stack(xs, axis=0)

    def prompt_state(k, heads, width):
        return from_t(new_p[k].reshape(depth, n_batch, 2, heads, HEAD_DIM, width))

    return (y_prompt, y_sample,
            prompt_state(0, N_KV_A, WINDOW_A), from_t(new_s[0]),
            prompt_state(1, HEADS_B, B_PAIRS[0][0]), from_t(new_s[1]),
            prompt_state(2, HEADS_B, B_PAIRS[1][0]), from_t(new_s[2]),
            prompt_state(3, HEADS_B, B_PAIRS[2][0]), from_t(new_s[3]),
            stack(new_c_p), jnp.transpose(stack(new_c_s), (0, 2, 1, 3)))
```

```python
import functools

import numpy as np
import jax
import jax.numpy as jnp
from jax import lax
from jax.experimental import pallas as pl
from jax.experimental.pallas import tpu as pltpu

F32 = jnp.float32
BF16 = jnp.bfloat16

D_MODEL = 1024
HEAD_DIM = 64
N_HEADS_A = 8
N_KV_A = 2
REP_A = N_HEADS_A // N_KV_A
WINDOW_A = 128
B_PAIRS = ((128, 1), (512, 4), (2048, 16))
N_GROUPS_B = 3
HEADS_B = 4
POOL_WINDOWS = (2, 4, 8, 16)
POOL_STATE = 15
WIDTH_A = N_HEADS_A * HEAD_DIM
WIDTH_KV_A = N_KV_A * HEAD_DIM
WIDTH_B = HEADS_B * HEAD_DIM
WIDTH_C = 256
BLOCK = 128
RMS_EPS = 1e-6
NEG_INF = -1e30
LANES = 128
SUBLANES = 8
VMEM_LIMIT = 56 * 1024 * 1024
DEC = 4
DEC_ROWS = 2 * DEC

COL_QA, COL_KA, COL_VA, COL_GA = 0, 512, 640, 768
COL_QB, COL_KB, COL_VB, COL_GB = 1280, 2048, 2816, 3584
COL_UC, COL_GC, COL_MG = 3840, 4096, 4352
D_IN = 7424
SEG_W = 768
QKV_W = 4 * SEG_W
GATE_W = 4096
KVT_A = 0
KVT_B = (256, 768, 1280)
KVT_ROWS = 1792


def _alibi_slopes(n):
    return 2.0 ** (-8.0 * (np.arange(n, dtype=np.float32) + 1.0) / n)


SLOPES_A = _alibi_slopes(N_HEADS_A)
SLOPES_B = _alibi_slopes(N_GROUPS_B * HEADS_B)


WIN_PREV_CUR = 0
WIN_CUR_NEXT = 1
WIN_MASKED_CUR = 2
N_WIN_KINDS = 3


def _band_table(slope, dist_scale, kind):
    qi = np.arange(BLOCK)[:, None]
    ki = np.arange(2 * BLOCK)[None, :]
    own = (ki < BLOCK) if kind == WIN_CUR_NEXT else (ki >= BLOCK)
    dist = (qi - ki) if kind == WIN_CUR_NEXT else (BLOCK + qi - ki)
    valid = (dist >= 0) & (dist <= BLOCK)
    if kind != WIN_PREV_CUR:
        valid &= own
    bias = -(np.float32(slope) * np.float32(dist_scale)) * dist.astype(np.float32)
    return np.where(valid, bias, np.float32(NEG_INF)).astype(np.float32)


def _tables_a():
    return np.stack([np.stack([_band_table(SLOPES_A[g * REP_A + p], 1, kind) for p in range(REP_A) for g in range(N_KV_A)])
                     for kind in (WIN_PREV_CUR, WIN_CUR_NEXT)])


def _tables_b():
    return np.stack([np.stack([np.stack([_band_table(SLOPES_B[g * HEADS_B + h], B_PAIRS[g][1], kind)
                                         for h in range(HEADS_B)]) for g in range(N_GROUPS_B)])
                     for kind in range(N_WIN_KINDS)])


def _decode_slopes():
    rows_a = np.array([SLOPES_A[(r % DEC_ROWS) // DEC * REP_A + r // DEC_ROWS] for r in range(REP_A * DEC_ROWS)], np.float32)
    rows_b = np.array([[SLOPES_B[g * HEADS_B + r // DEC] for r in range(HEADS_B * DEC)] for g in range(N_GROUPS_B)], np.float32)
    return (np.broadcast_to(rows_a[:, None], (REP_A * DEC_ROWS, LANES)).copy(),
            np.broadcast_to(rows_b[:, :, None], (N_GROUPS_B, HEADS_B * DEC, LANES)).copy())


def _cparams(sem):
    return pltpu.CompilerParams(dimension_semantics=sem, vmem_limit_bytes=VMEM_LIMIT)


def _const_spec(shape):
    nd = len(shape)
    return pl.BlockSpec(shape, lambda *_: (0,) * nd, pipeline_mode=pl.Buffered(1))


def _layer_spec(layer, shape):
    nd = len(shape)
    return pl.BlockSpec((1,) + shape, lambda *_: (layer,) + (0,) * nd, pipeline_mode=pl.Buffered(1))


def _rms_h(x, nw_ref):
    ms = jnp.mean(x * x, axis=-1, keepdims=True)
    return (x * lax.rsqrt(ms + RMS_EPS) * nw_ref[0]).astype(BF16)


def _proj(h, w_ref, lo, width):
    return jnp.dot(h, w_ref[0, :, lo:lo + width], preferred_element_type=F32)


def _store_gates(h, w_ref, gates_ref):
    o = 0
    for lo, width in ((COL_GA, WIDTH_A), (COL_GB, WIDTH_B), (COL_GC, WIDTH_C),
                      (COL_MG, D_MODEL), (COL_MG + D_MODEL, D_MODEL), (COL_MG + 2 * D_MODEL, D_MODEL)):
        gates_ref[:, o:o + width] = _proj(h, w_ref, lo, width).astype(BF16)
        o += width


def _inproj_prompt_tile(x, nw_ref, w_ref, qkva_ref, qkvb1_ref, qkvb2_ref, qkvb3_ref, gates_ref, uc_ref,
                        sa_ref, sb1_ref, sb2_ref, sb3_ref, sc_ref, zscr, tm):
    h = _rms_h(x, nw_ref)

    za = _proj(h, w_ref, COL_QA, SEG_W)
    qkva_ref[...] = za.astype(BF16)
    sa_ref[0, 0, 0] = za[tm - WINDOW_A:, WIDTH_A:WIDTH_A + WIDTH_KV_A].T
    sa_ref[0, 0, 1] = za[tm - WINDOW_A:, WIDTH_A + WIDTH_KV_A:].T

    for g, (s_ref, q_ref) in enumerate(((sb1_ref, qkvb1_ref), (sb2_ref, qkvb2_ref), (sb3_ref, qkvb3_ref))):
        width, dil = B_PAIRS[g]
        keep = min(width, tm)
        parts = [_proj(h, w_ref, col + g * WIDTH_B, WIDTH_B) for col in (COL_QB, COL_KB, COL_VB)]
        s_ref[0, 0, 0] = parts[1][tm - keep:, :].T
        s_ref[0, 0, 1] = parts[2][tm - keep:, :].T
        if dil == 1:
            for k, z in enumerate(parts):
                q_ref[:, k * WIDTH_B:(k + 1) * WIDTH_B] = z.astype(BF16)
            continue
        ncol = WIDTH_B // LANES
        for k, z in enumerate(parts):
            for c in range(ncol):
                zscr[k * ncol + c] = z[:, c * LANES:(c + 1) * LANES]
        for r in range(dil):
            for c in range(SEG_W // LANES):
                q_ref[0, r, :, c * LANES:(c + 1) * LANES] = zscr[c, pl.ds(r, tm // dil, stride=dil), :].astype(BF16)

    _store_gates(h, w_ref, gates_ref)
    zu = _proj(h, w_ref, COL_UC, WIDTH_C)
    uc_ref[...] = zu
    sc_ref[0] = zu[tm - POOL_STATE:, :]


N_OUT_IN = 12
N_INPROJ_OUT = 11


def _token_stage_body(*refs, has_out, has_in, final, n_prev, tm):
    refs = list(refs)
    out_in = refs[:N_OUT_IN] if has_out else refs[:1]
    pos = len(out_in)
    if has_in:
        nw_ref, w_ref = refs[pos:pos + 2]
        pos += 2
    pos += n_prev
    if has_out:
        y_ref = refs[pos]
        pos += 1
        x = _out_tile(*out_in, final=final)
        y_ref[...] = x
    else:
        x = out_in[0][...]
    if has_in:
        _inproj_prompt_tile(x, nw_ref, w_ref, *refs[pos:pos + N_INPROJ_OUT + 1], tm)


def _token_stage(n_batch, seq, x2d, out_args=None, in_args=None, prev_states=None, final=False, tm=256):
    nt = seq // tm
    m = n_batch * seq
    row = lambda n, i: (n * nt + i, 0)
    tile = lambda w: pl.BlockSpec((tm, w), row)
    args, in_specs, out_shape, out_specs = [x2d], [tile(D_MODEL)], [], []
    if out_args is not None:
        lo, oa, ob, diff, gates, wpa, wpb, wpc, wpool, ps, wout, fnw = out_args
        args += [oa, ob, diff, gates, wpa, wpb, wpc, wpool, ps, wout, fnw]
        in_specs += [tile(WIDTH_A), tile(WIDTH_B), tile(WIDTH_C), tile(GATE_W),
                     _layer_spec(lo, (WIDTH_A, D_MODEL)), _layer_spec(lo, (WIDTH_B, D_MODEL)),
                     _layer_spec(lo, (WIDTH_C, D_MODEL)), _layer_spec(lo, (WIDTH_C, WIDTH_C)),
                     _layer_spec(lo, (1, WIDTH_C)), _layer_spec(lo, (D_MODEL, D_MODEL)), _const_spec((1, D_MODEL))]
        out_shape.append(jax.ShapeDtypeStruct((m, D_MODEL), F32))
        out_specs.append(tile(D_MODEL))
    aliases = {}
    scratch = []
    if in_args is not None:
        li, nw, w, depth = in_args
        args += [nw, w]
        in_specs += [_layer_spec(li, (1, D_MODEL)), _layer_spec(li, (D_MODEL, D_IN))]
        b2_tiles = B_PAIRS[1][0] // tm
        state = lambda c, wd: jax.ShapeDtypeStruct((depth, n_batch, 2, c, wd), F32)
        first_state = len(out_shape) + 6
        out_shape += [
            jax.ShapeDtypeStruct((m, SEG_W), BF16),
            jax.ShapeDtypeStruct((m, SEG_W), BF16),
            jax.ShapeDtypeStruct((n_batch, 4, seq // 4, SEG_W), BF16),
            jax.ShapeDtypeStruct((n_batch, 16, seq // 16, SEG_W), BF16),
            jax.ShapeDtypeStruct((m, GATE_W), BF16),
            jax.ShapeDtypeStruct((m, WIDTH_C), F32),
            state(WIDTH_KV_A, WINDOW_A), state(WIDTH_B, B_PAIRS[0][0]),
            state(WIDTH_B, B_PAIRS[1][0]), state(WIDTH_B, B_PAIRS[2][0]),
            jax.ShapeDtypeStruct((n_batch, POOL_STATE, WIDTH_C), F32),
        ]
        out_specs += [
            tile(SEG_W), tile(SEG_W),
            pl.BlockSpec((1, 4, tm // 4, SEG_W), lambda n, i: (n, 0, i, 0)),
            pl.BlockSpec((1, 16, tm // 16, SEG_W), lambda n, i: (n, 0, i, 0)),
            tile(GATE_W), tile(WIDTH_C),
            pl.BlockSpec((1, 1, 2, WIDTH_KV_A, WINDOW_A), lambda n, i: (li, n, 0, 0, 0)),
            pl.BlockSpec((1, 1, 2, WIDTH_B, B_PAIRS[0][0]), lambda n, i: (li, n, 0, 0, 0)),
            pl.BlockSpec((1, 1, 2, WIDTH_B, tm), lambda n, i: (li, n, 0, 0, jnp.maximum(i - (nt - b2_tiles), 0))),
            pl.BlockSpec((1, 1, 2, WIDTH_B, tm), lambda n, i: (li, n, 0, 0, i)),
            pl.BlockSpec((1, POOL_STATE, WIDTH_C), lambda n, i: (n, 0, 0)),
        ]
        scratch = [pltpu.VMEM((SEG_W // LANES, tm, LANES), F32)]
        if prev_states is not None:
            aliases = {len(args) + k: first_state + k for k in range(len(prev_states))}
            in_specs += [pl.BlockSpec(memory_space=pl.ANY)] * len(prev_states)
            args += list(prev_states)
    return pl.pallas_call(
        functools.partial(_token_stage_body, has_out=out_args is not None, has_in=in_args is not None, final=final,
                          n_prev=len(aliases), tm=tm),
        grid=(n_batch, nt),
        in_specs=in_specs,
        out_specs=tuple(out_specs),
        out_shape=tuple(out_shape),
        scratch_shapes=scratch,
        input_output_aliases=aliases,
        compiler_params=_cparams(("arbitrary", "arbitrary")),
        name="token_stage",
    )(*args)


def _inproj_sample_body(x_ref, nw_ref, w_ref, qkv_ref, gates_ref, uc_ref, kvt_ref):
    h = _rms_h(x_ref[...], nw_ref)
    za = _proj(h, w_ref, COL_QA, SEG_W)
    qkv_ref[:, 0:SEG_W] = za
    kvt_ref[KVT_A:KVT_A + WIDTH_KV_A, :] = za[:, WIDTH_A:WIDTH_A + WIDTH_KV_A].T
    kvt_ref[KVT_A + WIDTH_KV_A:KVT_A + 2 * WIDTH_KV_A, :] = za[:, WIDTH_A + WIDTH_KV_A:].T
    for g in range(N_GROUPS_B):
        for k, col in enumerate((COL_QB, COL_KB, COL_VB)):
            o = (1 + g) * SEG_W + k * WIDTH_B
            z = _proj(h, w_ref, col + g * WIDTH_B, WIDTH_B)
            qkv_ref[:, o:o + WIDTH_B] = z
            if k > 0:
                r0 = KVT_B[g] + (k - 1) * WIDTH_B
                kvt_ref[r0:r0 + WIDTH_B, :] = z.T
    _store_gates(h, w_ref, gates_ref)
    uc_ref[...] = _proj(h, w_ref, COL_UC, WIDTH_C)


def _inproj_sample(layer, x2d, nw, w, tm=256):
    m = x2d.shape[0]
    row = lambda i: (i, 0)
    return pl.pallas_call(
        _inproj_sample_body,
        grid=(m // tm,),
        in_specs=[pl.BlockSpec((tm, D_MODEL), row), _layer_spec(layer, (1, D_MODEL)), _layer_spec(layer, (D_MODEL, D_IN))],
        out_specs=(pl.BlockSpec((tm, QKV_W), row), pl.BlockSpec((tm, GATE_W), row), pl.BlockSpec((tm, WIDTH_C), row),
                   pl.BlockSpec((KVT_ROWS, tm), lambda i: (0, i))),
        out_shape=(jax.ShapeDtypeStruct((m, QKV_W), F32), jax.ShapeDtypeStruct((m, GATE_W), BF16),
                   jax.ShapeDtypeStruct((m, WIDTH_C), F32), jax.ShapeDtypeStruct((KVT_ROWS, m), F32)),
        compiler_params=_cparams(("arbitrary",)),
        name="inproj_sample",
    )(x2d, nw, w)


def _lane_masks():
    lane = lax.broadcasted_iota(jnp.int32, (1, LANES), 1)
    lo = lane < HEAD_DIM
    mlo = jnp.where(lo, 1.0, 0.0).astype(BF16)
    mhi = jnp.where(lo, 0.0, 1.0).astype(BF16)
    return lo, mlo, mhi


def _attend(q, kwin, vwin, table, sink=None):
    s = lax.dot_general(q, kwin, (((1,), (1,)), ((), ())), preferred_element_type=F32) + table
    m = jnp.max(s, axis=-1, keepdims=True)
    if sink is not None:
        m = jnp.maximum(m, sink)
    e = jnp.exp(s - m)
    l = jnp.sum(e, axis=-1, keepdims=True)
    if sink is not None:
        l = l + jnp.exp(sink - m)
    acc = jnp.dot(e.astype(BF16), vwin, preferred_element_type=F32)
    return acc * (1.0 / l), m + jnp.log(l)


def _pool_diff(e_ref, rows, t0, pos0):
    cur = e_ref[pl.ds(t0, rows), :]
    acc = cur
    sums = {}
    for k in range(1, POOL_WINDOWS[-1]):
        acc = acc + e_ref[pl.ds(t0 - k, rows), :]
        if k + 1 in POOL_WINDOWS:
            sums[k + 1] = acc
    lane = lax.broadcasted_iota(jnp.int32, (rows, WIDTH_C), 1)
    pos = lax.broadcasted_iota(jnp.int32, (rows, WIDTH_C), 0) + pos0
    wsum = sums[POOL_WINDOWS[-1]]
    win = jnp.full((rows, WIDTH_C), POOL_WINDOWS[-1], jnp.int32)
    for gi in range(len(POOL_WINDOWS) - 2, -1, -1):
        sel = lane < (gi + 1) * HEAD_DIM
        wsum = jnp.where(sel, sums[POOL_WINDOWS[gi]], wsum)
        win = jnp.where(sel, POOL_WINDOWS[gi], win)
    cnt = jnp.minimum(pos + 1, win).astype(F32)
    return wsum / cnt - cur


def _attn_a_block(i, sink_ref, qkv_ref, tbl_ref, oa_ref):
    lo, mlo, mhi = _lane_masks()
    kc = WIDTH_A
    vc = WIDTH_A + WIDTH_KV_A
    first = (i == 0).astype(jnp.int32)
    kind = first * WIN_CUR_NEXT
    r0 = pl.multiple_of(i * BLOCK, BLOCK)
    w0 = pl.multiple_of((i - 1 + first) * BLOCK, BLOCK)
    kwin = qkv_ref[0, pl.ds(w0, 2 * BLOCK), kc:kc + WIDTH_KV_A]
    vwin = qkv_ref[0, pl.ds(w0, 2 * BLOCK), vc:vc + WIDTH_KV_A]
    for p in range(REP_A):
        qp = qkv_ref[0, pl.ds(r0, BLOCK), p * LANES:(p + 1) * LANES]
        o_lo, _ = _attend(qp * mlo, kwin, vwin, tbl_ref[kind, 2 * p], sink_ref[2 * p])
        o_hi, _ = _attend(qp * mhi, kwin, vwin, tbl_ref[kind, 2 * p + 1], sink_ref[2 * p + 1])
        oa_ref[0, :, p * LANES:(p + 1) * LANES] = jnp.where(lo, o_lo, o_hi)


def _pool_sequence(uc_ref, diff_ref, e_ref, seq):
    pad = 2 * SUBLANES
    e_ref[0:pad, :] = jnp.zeros((pad, WIDTH_C), F32)
    e_ref[pad:, :] = uc_ref[0]
    rows = 256
    for c in range(seq // rows):
        diff_ref[0, c * rows:(c + 1) * rows, :] = _pool_diff(e_ref, rows, pad + c * rows, c * rows)


def _attn_b_body(q1_ref, q2_ref, q3_ref, tbl_ref, ob_ref, o_nat, l_nat, o_stage, l_stage, *, seq):
    lo, mlo, mhi = _lane_masks()
    npair = WIDTH_B // LANES

    def block(q_ref, r0, w0, table, dst_o, dst_l):
        for p in range(npair):
            qp = q_ref[0, pl.ds(r0, BLOCK), p * LANES:(p + 1) * LANES]
            kwin = q_ref[0, pl.ds(w0, 2 * BLOCK), WIDTH_B + p * LANES:WIDTH_B + (p + 1) * LANES]
            vwin = q_ref[0, pl.ds(w0, 2 * BLOCK), 2 * WIDTH_B + p * LANES:2 * WIDTH_B + (p + 1) * LANES]
            o_lo, l_lo = _attend(qp * mlo, kwin, vwin, table(2 * p))
            o_hi, l_hi = _attend(qp * mhi, kwin, vwin, table(2 * p + 1))
            dst_o[p, pl.ds(r0, BLOCK), :] = jnp.where(lo, o_lo, o_hi)
            dst_l[p, pl.ds(r0, BLOCK), :] = jnp.where(lo, l_lo, l_hi)

    table = lambda kind, g: (lambda u: tbl_ref[kind, g, u])

    block(q1_ref, 0, 0, table(WIN_CUR_NEXT, 0), o_nat.at[0], l_nat.at[0])

    def loop1(i, c):
        r0 = pl.multiple_of(i * BLOCK, BLOCK)
        block(q1_ref, r0, r0 - BLOCK, table(WIN_PREV_CUR, 0), o_nat.at[0], l_nat.at[0])
        return c

    lax.fori_loop(1, seq // BLOCK, loop1, 0, unroll=True)

    sub = seq // B_PAIRS[1][1]
    for r in range(B_PAIRS[1][1]):
        block(q2_ref, r * sub, r * sub, table(WIN_CUR_NEXT, 1), o_stage, l_stage)

    def loop2(i, c):
        for r in range(B_PAIRS[1][1]):
            r0 = pl.multiple_of(r * sub + i * BLOCK, BLOCK)
            block(q2_ref, r0, r0 - BLOCK, table(WIN_PREV_CUR, 1), o_stage, l_stage)
        return c

    lax.fori_loop(1, sub // BLOCK, loop2, 0, unroll=True)
    for r in range(B_PAIRS[1][1]):
        for p in range(npair):
            o_nat[1, p, pl.ds(r, sub, stride=B_PAIRS[1][1]), :] = o_stage[p, r * sub:(r + 1) * sub, :]
            l_nat[1, p, pl.ds(r, sub, stride=B_PAIRS[1][1]), :] = l_stage[p, r * sub:(r + 1) * sub, :]

    dil3 = B_PAIRS[2][1]
    par = 8

    def loop3(rg, c):
        for k in range(par):
            r = rg * par + k
            last = r // (dil3 - 1)
            r0 = pl.multiple_of(r * BLOCK, BLOCK)
            w0 = pl.multiple_of((r - last) * BLOCK, BLOCK)
            block(q3_ref, r0, w0, (lambda u, last=last: tbl_ref[WIN_CUR_NEXT + last, 2, u]), o_stage, l_stage)
        return c

    lax.fori_loop(0, dil3 // par, loop3, 0)
    for r in range(dil3):
        for p in range(npair):
            o_nat[2, p, pl.ds(r, BLOCK, stride=dil3), :] = o_stage[p, r * BLOCK:(r + 1) * BLOCK, :]
            l_nat[2, p, pl.ds(r, BLOCK, stride=dil3), :] = l_stage[p, r * BLOCK:(r + 1) * BLOCK, :]

    rows = 256
    for c in range(seq // rows):
        for p in range(npair):
            ls = [l_nat[g, p, c * rows:(c + 1) * rows, :] for g in range(N_GROUPS_B)]
            mx = jnp.maximum(jnp.maximum(ls[0], ls[1]), ls[2])
            ws = [jnp.exp(x - mx) for x in ls]
            den = ws[0] + ws[1] + ws[2]
            num = sum(ws[g] * o_nat[g, p, c * rows:(c + 1) * rows, :] for g in range(N_GROUPS_B))
            ob_ref[0, c * rows:(c + 1) * rows, p * LANES:(p + 1) * LANES] = num * (1.0 / den)


def _attn_b(q1, q2, q3, tbl, n_batch, seq):
    npair = WIDTH_B // LANES
    return pl.pallas_call(
        functools.partial(_attn_b_body, seq=seq),
        grid=(n_batch,),
        in_specs=[pl.BlockSpec((1, seq, SEG_W), lambda n: (n, 0, 0))] * N_GROUPS_B
                 + [_const_spec((N_WIN_KINDS, N_GROUPS_B, HEADS_B, BLOCK, 2 * BLOCK))],
        out_specs=pl.BlockSpec((1, seq, WIDTH_B), lambda n: (n, 0, 0)),
        out_shape=jax.ShapeDtypeStruct((n_batch, seq, WIDTH_B), F32),
        scratch_shapes=[pltpu.VMEM((N_GROUPS_B, npair, seq, LANES), F32),
                        pltpu.VMEM((N_GROUPS_B, npair, seq, LANES), F32),
                        pltpu.VMEM((npair, seq, LANES), F32),
                        pltpu.VMEM((npair, seq, LANES), F32)],
        compiler_params=_cparams(("arbitrary",)),
        name="attn_b",
    )(q1, q2, q3, tbl)


def _sigmoid(x):
    return 0.5 * jnp.tanh(0.5 * x) + 0.5


def _out_tile(x_ref, oa_ref, ob_ref, diff_ref, g_ref, wpa_ref, wpb_ref, wpc_ref, wpool_ref, ps_ref, wout_ref,
              fnw_ref, *, final):
    def gate(lo, hi):
        return g_ref[:, lo:hi].astype(F32)

    def silu(g):
        return g * _sigmoid(g)

    o = WIDTH_A
    ga, gb, gc = gate(0, o), gate(o, o + WIDTH_B), gate(o + WIDTH_B, o + WIDTH_B + WIDTH_C)
    mbase = o + WIDTH_B + WIDTH_C
    pa = jnp.dot((oa_ref[...] * silu(ga)).astype(BF16), wpa_ref[0], preferred_element_type=F32)
    pb = jnp.dot((ob_ref[...] * silu(gb)).astype(BF16), wpb_ref[0], preferred_element_type=F32)
    oc = jnp.dot(diff_ref[...].astype(BF16), wpool_ref[0], preferred_element_type=F32) * ps_ref[0]
    pc = jnp.dot((oc * silu(gc)).astype(BF16), wpc_ref[0], preferred_element_type=F32)
    m = (_sigmoid(gate(mbase, mbase + D_MODEL)) * pa
         + _sigmoid(gate(mbase + D_MODEL, mbase + 2 * D_MODEL)) * pb
         + _sigmoid(gate(mbase + 2 * D_MODEL, mbase + 3 * D_MODEL)) * pc)
    y = x_ref[...] + jnp.dot(m.astype(BF16), wout_ref[0], preferred_element_type=F32)
    if final:
        ms = jnp.mean(y * y, axis=-1, keepdims=True)
        y = y * lax.rsqrt(ms + RMS_EPS) * fnw_ref[...]
    return y


def _decode_body(*refs, has_prev, seq):
    n_dec = 9
    outs = refs[n_dec + (4 if has_prev else 0):]
    i = pl.program_id(1)
    _decode_sequence(pl.program_id(0) * (seq // BLOCK) + i, refs[:n_dec], outs[:6])


def _attn_a_body(sink_ref, qkv_ref, uc_ref, tbl_ref, oa_ref, diff_ref, e_ref, *, seq):
    def blk(i, c):
        _attn_a_block(i, sink_ref, qkv_ref, tbl_ref, oa_ref.at[:, pl.ds(pl.multiple_of(i * BLOCK, BLOCK), BLOCK)])
        return c

    lax.fori_loop(0, seq // BLOCK, blk, 0, unroll=8)
    _pool_sequence(uc_ref, diff_ref, e_ref, seq)


def _attn_a(sink, qkva, uc, tbl, n_batch, seq):
    whole = lambda w: pl.BlockSpec((1, seq, w), lambda n: (n, 0, 0))
    return pl.pallas_call(
        functools.partial(_attn_a_body, seq=seq),
        grid=(n_batch,),
        in_specs=[pl.BlockSpec(memory_space=pltpu.SMEM), whole(SEG_W), whole(WIDTH_C), _const_spec(tbl.shape)],
        out_specs=(whole(WIDTH_A), whole(WIDTH_C)),
        out_shape=(jax.ShapeDtypeStruct((n_batch, seq, WIDTH_A), F32), jax.ShapeDtypeStruct((n_batch, seq, WIDTH_C), F32)),
        scratch_shapes=[pltpu.VMEM((seq + 2 * SUBLANES, WIDTH_C), F32)],
        compiler_params=_cparams(("arbitrary",)),
        name="attn_a_pool",
    )(sink, qkva, uc, tbl)


def _decode_sequence(n, in_refs, out_refs):
    qkv_ref, kvt_ref, sink_ref, sla_ref, slb_ref, sa_ref, sb1_ref, sb2_ref, sb3_ref = in_refs
    oa_ref, ob_ref, na_ref, nb1_ref, nb2_ref, nb3_ref = out_refs
    new_shift = (LANES - DEC) - (n % (LANES // DEC_ROWS)) * DEC_ROWS

    def shifted_state(s_ref, n_ref, kv, rows, width, kvt_row):
        x = s_ref[0, 0, kv].reshape(rows, width)
        rolled = pltpu.roll(x, width - DEC, axis=1)
        new_t = pltpu.roll(kvt_ref[kvt_row:kvt_row + rows, :], new_shift, axis=1)
        lane = lax.broadcasted_iota(jnp.int32, (rows, LANES), 1)
        last = jnp.where(lane < LANES - DEC, rolled[:, width - LANES:], new_t)
        heads = rows // HEAD_DIM
        if width > LANES:
            n_ref[0, 0, kv, :, :, 0:width - LANES] = rolled[:, 0:width - LANES].reshape(heads, HEAD_DIM, width - LANES)
        n_ref[0, 0, kv, :, :, width - LANES:width] = last.reshape(heads, HEAD_DIM, LANES)
        return x

    def attend(q, kt, vt, knew, vnew, live, slopes, width, dil, sink):
        rows = q.shape[0]
        r_i = lax.broadcasted_iota(jnp.int32, (rows, width), 0)
        t_i = lax.broadcasted_iota(jnp.int32, (rows, width), 1)
        jq = r_i & (DEC - 1)
        s = jnp.dot(q.astype(BF16), kt.astype(BF16), preferred_element_type=F32)
        dist = (float(width) + jq.astype(F32)) - t_i.astype(F32)
        valid = (t_i >= jq) if dil == 1 else ((t_i & (dil - 1)) == jq)
        s = jnp.where(valid, s - slopes * dist, NEG_INF)
        jq1 = jq[:, :1].astype(F32)
        s_new = []
        if dil == 1:
            for j2 in range(DEC):
                kb = jnp.broadcast_to(knew[j2:j2 + 1, :], q.shape)
                d2 = jq1 - float(j2)
                s_new.append(jnp.where(d2 >= 0.0, jnp.sum(q * kb, axis=-1, keepdims=True) - slopes * d2, NEG_INF))
        else:
            s_new.append(jnp.sum(q * knew, axis=-1, keepdims=True))
        m = jnp.max(s, axis=-1, keepdims=True)
        for sn in s_new:
            m = jnp.maximum(m, sn)
        if sink is not None:
            m = jnp.maximum(m, sink)
        e = jnp.exp(s - m)
        l = jnp.sum(e, axis=-1, keepdims=True)
        acc = lax.dot_general(e.astype(BF16), vt.astype(BF16), (((1,), (1,)), ((), ())), preferred_element_type=F32)
        for j2, sn in enumerate(s_new):
            en = jnp.exp(sn - m)
            l = l + en
            vb = jnp.broadcast_to(vnew[j2:j2 + 1, :], q.shape) if dil == 1 else vnew
            acc = acc + en * vb
        if sink is not None:
            l = l + jnp.exp(sink - m)
        return jnp.where(live, acc * (1.0 / l), 0.0), m + jnp.log(l)

    kt = shifted_state(sa_ref, na_ref, 0, WIDTH_KV_A, WINDOW_A, KVT_A)
    vt = shifted_state(sa_ref, na_ref, 1, WIDTH_KV_A, WINDOW_A, KVT_A + WIDTH_KV_A)
    ra = REP_A * DEC_ROWS
    row = lax.broadcasted_iota(jnp.int32, (ra, LANES), 0)
    lane = lax.broadcasted_iota(jnp.int32, (ra, LANES), 1)
    live_a = ((row & (DEC_ROWS - 1)) < DEC) == (lane < HEAD_DIM)
    q32 = jnp.concatenate([qkv_ref[0, :, p * LANES:(p + 1) * LANES] for p in range(REP_A)], axis=0)
    q32 = jnp.where(live_a, q32, 0.0)
    knew = jnp.concatenate([qkv_ref[0, :, WIDTH_A:WIDTH_A + WIDTH_KV_A]] * REP_A, axis=0)
    vnew = jnp.concatenate([qkv_ref[0, :, WIDTH_A + WIDTH_KV_A:SEG_W]] * REP_A, axis=0)
    o32, _ = attend(q32, kt, vt, knew, vnew, live_a, sla_ref[:, 0:1], WINDOW_A, 1, sink_ref[0])
    for p in range(REP_A):
        r8 = o32[p * DEC_ROWS:(p + 1) * DEC_ROWS]
        oa_ref[0, :, p * LANES:(p + 1) * LANES] = r8 + pltpu.roll(r8, DEC, axis=0)

    rb = HEADS_B * DEC
    row = lax.broadcasted_iota(jnp.int32, (rb, WIDTH_B), 0)
    lane = lax.broadcasted_iota(jnp.int32, (rb, WIDTH_B), 1)
    live_b = (lane // HEAD_DIM) == (row // DEC)
    o_g, lse_g = [], []
    for g, (s_ref, n_ref) in enumerate(((sb1_ref, nb1_ref), (sb2_ref, nb2_ref), (sb3_ref, nb3_ref))):
        width, dil = B_PAIRS[g]
        base = (1 + g) * SEG_W
        kt = shifted_state(s_ref, n_ref, 0, WIDTH_B, width, KVT_B[g])
        vt = shifted_state(s_ref, n_ref, 1, WIDTH_B, width, KVT_B[g] + WIDTH_B)
        q16 = jnp.where(live_b, jnp.concatenate([qkv_ref[0, :, base:base + WIDTH_B]] * 2, axis=0), 0.0)
        knew = jnp.concatenate([qkv_ref[0, :, base + WIDTH_B:base + 2 * WIDTH_B]] * 2, axis=0)
        vnew = jnp.concatenate([qkv_ref[0, :, base + 2 * WIDTH_B:base + 3 * WIDTH_B]] * 2, axis=0)
        o16, lse = attend(q16, kt, vt, knew, vnew, live_b, slb_ref[g, :, 0:1], width, dil, None)
        o_g.append(o16)
        lse_g.append(lse)
    mx = jnp.maximum(jnp.maximum(lse_g[0], lse_g[1]), lse_g[2])
    ws = [jnp.exp(x - mx) for x in lse_g]
    o16 = (ws[0] * o_g[0] + ws[1] * o_g[1] + ws[2] * o_g[2]) * (1.0 / (ws[0] + ws[1] + ws[2]))
    x8 = o16[0:DEC_ROWS] + o16[DEC_ROWS:]
    ob_ref[0] = x8 + pltpu.roll(x8, DEC, axis=0)


def _decode(layer, qkv8, kvt, sink32, slopes_a, slopes_b, states, prev):
    nseq = qkv8.shape[0]
    nblk = LANES // DEC_ROWS
    n_batch = nseq // nblk
    seq = nblk * BLOCK
    dseq = lambda b, i: b * nblk + i
    st = lambda a: pl.BlockSpec((1, 1) + a.shape[2:], lambda b, i: (layer, dseq(b, i), 0, 0, 0, 0))
    per_seq = lambda w: pl.BlockSpec((1, DEC_ROWS, w), lambda b, i: (dseq(b, i), 0, 0))
    in_specs = [per_seq(QKV_W),
                pl.BlockSpec((KVT_ROWS, LANES), lambda b, i: (0, dseq(b, i) // (LANES // DEC_ROWS))),
                _layer_spec(layer, sink32.shape[1:]), _const_spec(slopes_a.shape), _const_spec(slopes_b.shape)]
    in_specs += [st(a) for a in states]
    args = [qkv8, kvt, sink32, slopes_a, slopes_b, *states]
    aliases = {}
    if prev is not None:
        in_specs += [pl.BlockSpec(memory_space=pl.ANY)] * len(prev)
        aliases = {len(args) + k: 2 + k for k in range(len(prev))}
        args += list(prev)
    out_specs = (per_seq(WIDTH_A), per_seq(WIDTH_B)) + tuple(st(a) for a in states)
    out_shape = ((jax.ShapeDtypeStruct((nseq, DEC_ROWS, WIDTH_A), F32), jax.ShapeDtypeStruct((nseq, DEC_ROWS, WIDTH_B), F32))
                 + tuple(jax.ShapeDtypeStruct(a.shape, F32) for a in states))
    return pl.pallas_call(
        functools.partial(_decode_body, has_prev=prev is not None, seq=seq),
        grid=(n_batch, nblk),
        in_specs=in_specs,
        out_specs=out_specs,
        out_shape=out_shape,
        input_output_aliases=aliases,
        compiler_params=_cparams(("arbitrary", "arbitrary")),
        name="decode_attn",
    )(*args)


def _pool_sample_body(sc_ref, u_ref, diff_ref, nc_ref):
    rows = [sc_ref[0, k] for k in range(POOL_STATE)] + [u_ref[j] for j in range(DEC)]
    lane = lax.broadcasted_iota(jnp.int32, rows[0].shape, 1)
    for j in range(DEC):
        t = POOL_STATE + j
        acc = rows[t]
        sums = {}
        for k in range(1, POOL_WINDOWS[-1]):
            acc = acc + rows[t - k]
            if k + 1 in POOL_WINDOWS:
                sums[k + 1] = acc
        mean = sums[POOL_WINDOWS[-1]] / float(POOL_WINDOWS[-1])
        for gi in range(len(POOL_WINDOWS) - 2, -1, -1):
            mean = jnp.where(lane < (gi + 1) * HEAD_DIM, sums[POOL_WINDOWS[gi]] / float(POOL_WINDOWS[gi]), mean)
        diff_ref[j] = mean - rows[t]
    for k in range(POOL_STATE):
        nc_ref[k] = rows[k + DEC]


def _pool_sample(layer, sc_t, u_slabs):
    nseq = u_slabs.shape[1]
    return pl.pallas_call(
        _pool_sample_body,
        grid=(1,),
        in_specs=[pl.BlockSpec((1, POOL_STATE, nseq, WIDTH_C), lambda i: (layer, 0, 0, 0)),
                  pl.BlockSpec((DEC, nseq, WIDTH_C), lambda i: (0, 0, 0))],
        out_specs=(pl.BlockSpec((DEC, nseq, WIDTH_C), lambda i: (0, 0, 0)),
                   pl.BlockSpec((POOL_STATE, nseq, WIDTH_C), lambda i: (0, 0, 0))),
        out_shape=(jax.ShapeDtypeStruct((DEC, nseq, WIDTH_C), F32), jax.ShapeDtypeStruct((POOL_STATE, nseq, WIDTH_C), F32)),
        compiler_params=_cparams(("arbitrary",)),
        name="pool_sample",
    )(sc_t, u_slabs)


def _head_pairs(x, axis):
    shp = x.shape
    x = x.reshape(shp[:axis] + (N_KV_A, REP_A, HEAD_DIM) + shp[axis + 1:])
    x = jnp.swapaxes(x, axis, axis + 1)
    return x.reshape(shp)


def _prep_weights(w_in, w_proj_a, w_pool):
    scale = HEAD_DIM ** -0.5
    pieces = [
        (_head_pairs(w_in[:, :, COL_QA:COL_KA], 2) * scale).astype(BF16),
        w_in[:, :, COL_KA:COL_GA].astype(BF16),
        _head_pairs(w_in[:, :, COL_GA:COL_QB], 2).astype(BF16),
        (w_in[:, :, COL_QB:COL_KB] * scale).astype(BF16),
        w_in[:, :, COL_KB:].astype(BF16),
    ]
    w_bf = jnp.concatenate(pieces, axis=-1)
    wpa_p = _head_pairs(w_proj_a, 1).astype(BF16)
    depth = w_pool.shape[0]
    ng = len(POOL_WINDOWS)
    eye = jnp.eye(ng, dtype=w_pool.dtype)
    wpool_bd = (w_pool[:, :, :, None, :] * eye[None, :, None, :, None]).reshape(depth, WIDTH_C, WIDTH_C).astype(BF16)
    return w_bf, wpa_p, wpool_bd


def kernel(x_prompt, x_sample, state_a, state_b1, state_b2, state_b3, state_c, norm_w, final_norm_w, w_in, sink_a,
           w_proj_a, w_proj_b, w_proj_c, w_pool, pool_scale, w_out):
    n_batch, seq, _ = x_prompt.shape
    nseq, dec, _ = x_sample.shape
    depth = w_in.shape[0]
    assert dec == DEC

    w_bf, wpa_p, wpool_bd = _prep_weights(w_in, w_proj_a, w_pool)
    wpb, wpc, wout = w_proj_b.astype(BF16), w_proj_c.astype(BF16), w_out.astype(BF16)
    nw3 = norm_w.reshape(depth, 1, D_MODEL)
    ps3 = pool_scale.reshape(depth, 1, WIDTH_C)
    fnw = final_norm_w.reshape(1, D_MODEL)
    tbl_a = jnp.asarray(_tables_a())
    tbl_b = jnp.asarray(_tables_b())
    sl_a, sl_b = (jnp.asarray(t) for t in _decode_slopes())
    sink_prompt = sink_a.reshape(depth, N_KV_A, REP_A).transpose(0, 2, 1).reshape(depth, N_HEADS_A)
    sink_dec = jnp.repeat(sink_prompt, DEC, axis=1)[..., None]

    to_t = lambda s: jnp.transpose(s, (0, 1, 2, 4, 5, 3))
    from_t = lambda s: jnp.transpose(s, (0, 1, 2, 5, 3, 4))
    states = tuple(to_t(s) for s in (state_a, state_b1, state_b2, state_b3))
    sc_t = jnp.transpose(state_c, (0, 2, 1, 3))

    hp = x_prompt.reshape(n_batch * seq, D_MODEL)
    hs = jnp.concatenate([x_sample, x_sample], axis=1).reshape(nseq * DEC_ROWS, D_MODEL)
    new_c_p, new_c_s = [], []
    new_s = None
    out_w = (wpa_p, wpb, wpc, wpool_bd, ps3, wout, fnw)
    stage = _token_stage(n_batch, seq, hp, in_args=(0, nw3, w_bf, depth))
    for l in range(depth):
        final = l == depth - 1
        qkva, qkvb1, qkvb2, qkvb3, gates, uc = stage[:6]
        new_p = stage[6:10]
        new_c_p.append(stage[10])
        qkv_s, gates_s, uc_s, kvt = _inproj_sample(l, hs, nw3, w_bf)
        outs = _decode(l, qkv_s.reshape(nseq, DEC_ROWS, QKV_W), kvt, sink_dec, sl_a, sl_b, states, new_s)
        oa_s, ob_s, new_s = outs[0], outs[1], outs[2:6]
        oa, diff = _attn_a(sink_prompt[l], qkva.reshape(n_batch, seq, SEG_W), uc.reshape(n_batch, seq, WIDTH_C),
                           tbl_a, n_batch, seq)
        ob = _attn_b(*(q.reshape(n_batch, seq, SEG_W) for q in (qkvb1, qkvb2, qkvb3)), tbl_b, n_batch, seq)
        out_args = (l, oa.reshape(-1, WIDTH_A), ob.reshape(-1, WIDTH_B), diff.reshape(-1, WIDTH_C), gates) + out_w
        if final:
            hp = _token_stage(n_batch, seq, hp, out_args=out_args, final=True)[0]
        else:
            res = _token_stage(n_batch, seq, hp, out_args=out_args, in_args=(l + 1, nw3, w_bf, depth), prev_states=new_p)
            hp, stage = res[0], res[1:]
        u_slabs = uc_s.reshape(nseq, DEC_ROWS, WIDTH_C)[:, :DEC].transpose(1, 0, 2)
        diff_t, nc_t = _pool_sample(l, sc_t, u_slabs)
        diff_s = jnp.tile(diff_t.transpose(1, 0, 2), (1, 2, 1)).reshape(nseq * DEC_ROWS, WIDTH_C)
        out_args_s = (l, oa_s.reshape(-1, WIDTH_A), ob_s.reshape(-1, WIDTH_B), diff_s, gates_s) + out_w
        hs = _token_stage(1, nseq * DEC_ROWS, hs, out_args=out_args_s, final=final)[0]
        new_c_s.append(nc_t)

    y_prompt = hp.reshape(n_batch, seq, D_MODEL)
    y_sample = hs.reshape(nseq, DEC_ROWS, D_MODEL)[:, :DEC]
    stack = lambda xs: jnp.stack(xs, axis=0)

    def prompt_state(k, heads, width):
        return from_t(new_p[k].reshape(depth, n_batch, 2, heads, HEAD_DIM, width))

    return (y_prompt, y_sample,
            prompt_state(0, N_KV_A, WINDOW_A), from_t(new_s[0]),
            prompt_state(1, HEADS_B, B_PAIRS[0][0]), from_t(new_s[1]),
            prompt_state(2, HEADS_B, B_PAIRS[1][0]), from_t(new_s[2]),
            prompt_state(3, HEADS_B, B_PAIRS[2][0]), from_t(new_s[3]),
            stack(new_c_p), jnp.transpose(stack(new_c_s), (0, 2, 1, 3)))
```
